```python
import math
import jax
import jax.numpy as jnp
from jax import lax
import numpy as np

D_MODEL = 1024
BATCH = 32
SEQ = 2048
DEPTH = 4

CTX_LEN = 256
GRID_W = 64
NORM_EPS = 1e-6

S5_WIDTH = D_MODEL // 2
S5_GROUP_CH = 16
S5_GROUPS = S5_WIDTH // S5_GROUP_CH
S5_STATE = 64
S5_DT_MIN = 1e-3
S5_DT_MAX = 1e-1

GM_WIDTH = D_MODEL // 2
GM_HEADS = 8
GM_HEAD_DIM = GM_WIDTH // GM_HEADS
GM_CHUNK = 128

HYB_IN = S5_WIDTH + 2 * GM_WIDTH
HYB_OUT = S5_WIDTH + GM_WIDTH

MLA_HEADS = D_MODEL // 128
MLA_Q_LORA = 3 * D_MODEL // 8
MLA_KV_LORA = D_MODEL // 4
MLA_NOPE = 128
MLA_ROPE = 64
MLA_V = 128
MLA_IN = MLA_Q_LORA + MLA_KV_LORA + MLA_ROPE
MLA_SCALE = (MLA_NOPE + MLA_ROPE) ** -0.5
Q_BLOCK = 128
ROPE_BASE = 10000.0

N_EXPERTS = 16
N_GROUPS = 4
GROUP_SIZE = N_EXPERTS // N_GROUPS
TOP_K = 2
EXPERT_FF = D_MODEL // 2

kernel_name = 'hybrid_s5_gmlp_mla_moe_dit'


def rmsnorm(x, g):
    xf = x.astype(jnp.float32)
    y = xf * lax.rsqrt(jnp.mean(xf * xf, axis=-1, keepdims=True) + NORM_EPS)
    return (y * g.astype(jnp.float32)).astype(x.dtype)


def layernorm(x, g, b):
    xf = x.astype(jnp.float32)
    mu = jnp.mean(xf, axis=-1, keepdims=True)
    xc = xf - mu
    var = jnp.mean(xc * xc, axis=-1, keepdims=True)
    return (xc * lax.rsqrt(var + NORM_EPS) * g.astype(jnp.float32) + b.astype(jnp.float32)).astype(x.dtype)


def adaln(cond, w, b, n_chunks):
    m = jax.nn.silu(cond) @ w[:, :n_chunks * D_MODEL] + b[:n_chunks * D_MODEL]
    return jnp.split(m, n_chunks, axis=-1)


def modulate(x, g, shift, scale):
    return rmsnorm(x, g) * (1.0 + scale) + shift


def _rope_1d(x, pos):
    half = x.shape[-1] // 2
    freqs = ROPE_BASE ** (-jnp.arange(half, dtype=jnp.float32) / half)
    ang = pos[:, None] * freqs[None, :]
    shape = (ang.shape[0],) + (1,) * (x.ndim - 3) + (half,)
    cos = jnp.cos(ang).reshape(shape).astype(x.dtype)
    sin = jnp.sin(ang).reshape(shape).astype(x.dtype)
    x1, x2 = x[..., :half], x[..., half:]
    return jnp.concatenate([x1 * cos - x2 * sin, x1 * sin + x2 * cos], axis=-1)


def axial_rope(x, pos_row, pos_col):
    h = x.shape[-1] // 2
    return jnp.concatenate([_rope_1d(x[..., :h], pos_row), _rope_1d(x[..., h:], pos_col)], axis=-1)


def _cmul(ar, ai, br, bi):
    return ar * br - ai * bi, ar * bi + ai * br


def _ssm_combine(e1, e2):
    a1r, a1i, b1r, b1i = e1
    a2r, a2i, b2r, b2i = e2
    ar, ai = _cmul(a2r, a2i, a1r, a1i)
    br, bi = _cmul(a2r, a2i, b1r, b1i)
    return ar, ai, br + b2r, bi + b2i


def s5_discretize(a_re, a_im, log_dt, b_re, b_im):
    f32 = jnp.float32
    a_re, a_im, b_re, b_im = a_re.astype(f32), a_im.astype(f32), b_re.astype(f32), b_im.astype(f32)
    dt = jnp.exp(log_dt.astype(f32))[:, None]
    mag = jnp.exp(dt * a_re)
    ab_re = mag * jnp.cos(dt * a_im)
    ab_im = mag * jnp.sin(dt * a_im)
    nr, ni = ab_re - 1.0, ab_im
    den = a_re * a_re + a_im * a_im
    f_re = (nr * a_re + ni * a_im) / den
    f_im = (ni * a_re - nr * a_im) / den
    bb_re, bb_im = _cmul(f_re[..., None], f_im[..., None], b_re, b_im)
    return ab_re, ab_im, bb_re, bb_im


def s5_states(u, ab_re, ab_im, bb_re, bb_im, h0, reverse):
    if reverse:
        u = jnp.flip(u, axis=1)
    bu_re = jnp.einsum('blgp,gnp->blgn', u, bb_re)
    bu_im = jnp.einsum('blgp,gnp->blgn', u, bb_im)
    if h0 is not None:
        inj_re, inj_im = _cmul(ab_re, ab_im, h0[0], h0[1])
        bu_re = bu_re.at[:, 0].add(inj_re)
        bu_im = bu_im.at[:, 0].add(inj_im)
    shape = (1, u.shape[1]) + ab_re.shape
    elems = (jnp.broadcast_to(ab_re, shape), jnp.broadcast_to(ab_im, shape), bu_re, bu_im)
    _, _, x_re, x_im = lax.associative_scan(_ssm_combine, elems, axis=1)
    if reverse:
        x_re, x_im = jnp.flip(x_re, axis=1), jnp.flip(x_im, axis=1)
    return x_re, x_im


def s5_readout(x_re, x_im, c_re, c_im):
    return jnp.einsum('blgn,gpn->blgp', x_re, c_re) - jnp.einsum('blgn,gpn->blgp', x_im, c_im)


def s5_mixer(u_ctx, u_lat, a_re, a_im, log_dt, b_re, b_im, c_re, c_im, d, glu_w, glu_b, need_ctx):
    f32 = jnp.float32

    def groups(u):
        return u.astype(f32).reshape(u.shape[0], u.shape[1], S5_GROUPS, S5_GROUP_CH)

    uc, ul = groups(u_ctx), groups(u_lat)
    d_g = d.astype(f32).reshape(S5_GROUPS, S5_GROUP_CH)
    y_lat = ul * d_g
    y_ctx = uc * d_g if need_ctx else None
    for direction, reverse in ((0, False), (1, True)):
        ab_re, ab_im, bb_re, bb_im = s5_discretize(a_re[direction], a_im[direction], log_dt[direction],
                                                   b_re[direction], b_im[direction])
        xc_re, xc_im = s5_states(uc, ab_re, ab_im, bb_re, bb_im, None, reverse)
        end = 0 if reverse else -1
        xl_re, xl_im = s5_states(ul, ab_re, ab_im, bb_re, bb_im, (xc_re[:, end], xc_im[:, end]), reverse)
        cr, ci = c_re[direction].astype(f32), c_im[direction].astype(f32)
        y_lat = y_lat + s5_readout(xl_re, xl_im, cr, ci)
        if need_ctx:
            y_ctx = y_ctx + s5_readout(xc_re, xc_im, cr, ci)

    def glu(y, dtype):
        y = jax.nn.gelu(y.reshape(y.shape[0], y.shape[1], S5_WIDTH).astype(dtype))
        return y * jax.nn.sigmoid(y @ glu_w + glu_b)

    y_lat = glu(y_lat, u_lat.dtype)
    y_ctx = glu(y_ctx, u_ctx.dtype) if need_ctx else None
    return y_ctx, y_lat


def chunk_gating(z, ln_g, ln_b, ws, bs):
    Bn, L, _ = z.shape
    z = jax.nn.gelu(z)
    u, v = z[..., :GM_WIDTH], z[..., GM_WIDTH:]
    v = layernorm(v, ln_g, ln_b).reshape(Bn, L // GM_CHUNK, GM_CHUNK, GM_HEADS, GM_HEAD_DIM)
    v = jnp.einsum('hij,bcjhd->bcihd', ws, v) + bs.T[:, :, None]
    return u * v.reshape(Bn, L, GM_WIDTH)


def hybrid_mixer(h_ctx, h_lat, in_w, out_w, a_re, a_im, log_dt, b_re, b_im, c_re, c_im, d, glu_w, glu_b,
                 ln_g, ln_b, ws, bs, need_ctx):
    z_lat = h_lat @ in_w
    z_ctx = h_ctx @ (in_w if need_ctx else in_w[:, :S5_WIDTH])
    s5_ctx, s5_lat = s5_mixer(z_ctx[..., :S5_WIDTH], z_lat[..., :S5_WIDTH], a_re, a_im, log_dt, b_re, b_im,
                              c_re, c_im, d, glu_w, glu_b, need_ctx)
    y_lat = jnp.concatenate([s5_lat, chunk_gating(z_lat[..., S5_WIDTH:], ln_g, ln_b, ws, bs)], axis=-1) @ out_w
    y_ctx = None
    if need_ctx:
        y_ctx = jnp.concatenate([s5_ctx, chunk_gating(z_ctx[..., S5_WIDTH:], ln_g, ln_b, ws, bs)], axis=-1) @ out_w
    return y_ctx, y_lat


def mla_queries(cq, q_norm_g, wq_b, qn_g, qr_g, rope_pos):
    Bn, L, _ = cq.shape
    q = (rmsnorm(cq, q_norm_g) @ wq_b).reshape(Bn, L, MLA_HEADS, MLA_NOPE + MLA_ROPE)
    q_nope = rmsnorm(q[..., :MLA_NOPE], qn_g)
    q_rope = rmsnorm(q[..., MLA_NOPE:], qr_g)
    if rope_pos is not None:
        q_rope = axial_rope(q_rope, rope_pos[0], rope_pos[1])
    return q_nope, q_rope


def mla_keys_values(ckv, k_rope, kv_norm_g, wkv_b, kn_g, kr_g, rope_pos):
    Bn, L, _ = ckv.shape
    kv = (rmsnorm(ckv, kv_norm_g) @ wkv_b).reshape(Bn, L, MLA_HEADS, MLA_NOPE + MLA_V)
    k_nope = rmsnorm(kv[..., :MLA_NOPE], kn_g)
    k_rope = rmsnorm(k_rope, kr_g)
    if rope_pos is not None:
        k_rope = axial_rope(k_rope, rope_pos[0], rope_pos[1])
    return k_nope, k_rope, kv[..., MLA_NOPE:]


def mla_attend(q_nope, q_rope, k_nope, k_rope, v):
    s = jnp.einsum('bqhd,bkhd->bhqk', q_nope, k_nope) + jnp.einsum('bqhr,bkr->bhqk', q_rope, k_rope)
    p = jax.nn.softmax(s.astype(jnp.float32) * MLA_SCALE, axis=-1).astype(v.dtype)
    return jnp.einsum('bhqk,bkhd->bqhd', p, v)


def mla_mixer(h_ctx, h_lat, in_w, q_norm_g, kv_norm_g, wq_b, wkv_b, qn_g, qr_g, kn_g, kr_g, out_w,
              rope_pos, need_ctx):
    Bn, L, _ = h_lat.shape
    kv_end = MLA_Q_LORA + MLA_KV_LORA
    z_lat = h_lat @ in_w
    z_ctx = h_ctx @ (in_w if need_ctx else in_w[:, MLA_Q_LORA:])
    zc_kv = z_ctx[..., -(MLA_KV_LORA + MLA_ROPE):]
    kn_c, kr_c, v_c = mla_keys_values(zc_kv[..., :MLA_KV_LORA], zc_kv[..., MLA_KV_LORA:],
                                      kv_norm_g, wkv_b, kn_g, kr_g, None)
    kn_l, kr_l, v_l = mla_keys_values(z_lat[..., MLA_Q_LORA:kv_end], z_lat[..., kv_end:],
                                      kv_norm_g, wkv_b, kn_g, kr_g, rope_pos)
    qn_l, qr_l = mla_queries(z_lat[..., :MLA_Q_LORA], q_norm_g, wq_b, qn_g, qr_g, rope_pos)
    k_nope = jnp.concatenate([kn_c, kn_l], axis=1)
    k_rope = jnp.concatenate([kr_c, kr_l], axis=1)
    v = jnp.concatenate([v_c, v_l], axis=1)
    nblk = L // Q_BLOCK

    def to_blocks(t):
        return t.reshape((Bn, nblk, Q_BLOCK) + t.shape[2:]).swapaxes(0, 1)

    o = lax.map(lambda qb: mla_attend(qb[0], qb[1], k_nope, k_rope, v), (to_blocks(qn_l), to_blocks(qr_l)))
    y_lat = o.swapaxes(0, 1).reshape(Bn, L, MLA_HEADS * MLA_V) @ out_w
    y_ctx = None
    if need_ctx:
        qn_c, qr_c = mla_queries(z_ctx[..., :MLA_Q_LORA], q_norm_g, wq_b, qn_g, qr_g, None)
        o_c = mla_attend(qn_c, qr_c, kn_c, kr_c, v_c)
        y_ctx = o_c.reshape(Bn, o_c.shape[1], MLA_HEADS * MLA_V) @ out_w
    return y_ctx, y_lat


def moe(h, router_w, router_bias, w1, w3, w2):
    scores = jax.nn.sigmoid(jnp.einsum('bld,de->ble', h, router_w).astype(jnp.float32))
    sel = scores + router_bias.astype(jnp.float32)
    grp = sel.reshape(sel.shape[:-1] + (N_GROUPS, GROUP_SIZE))
    grp_score = lax.top_k(grp, 2)[0].sum(axis=-1)
    best = jnp.argmax(grp_score, axis=-1)
    in_group = best[..., None] == (jnp.arange(N_EXPERTS) // GROUP_SIZE)
    _, idx = lax.top_k(jnp.where(in_group, sel, -jnp.inf), TOP_K)
    w = jnp.take_along_axis(scores, idx, axis=-1)
    w = w / jnp.sum(w, axis=-1, keepdims=True)
    gates = jnp.sum(jax.nn.one_hot(idx, N_EXPERTS, dtype=jnp.float32) * w[..., None], axis=-2).astype(h.dtype)
    out = jnp.zeros_like(h)
    for e in range(N_EXPERTS):
        hid = jax.nn.silu(h @ w1[e]) * (h @ w3[e])
        out = out + gates[..., e:e + 1] * (hid @ w2[e])
    return out


def setup_inputs(seed: int = 0) -> dict:
    key = jax.random.key(seed)
    ks = iter(jax.random.split(key, 48))
    f32 = jnp.float32
    n_even = (DEPTH + 1) // 2
    n_odd = DEPTH // 2

    def nrm(shape, scale):
        return scale * jax.random.normal(next(ks), shape, f32)

    s5_n = jnp.arange(S5_STATE, dtype=f32)
    return {
        'x': nrm((BATCH, SEQ, D_MODEL), 1.0),
        'c': nrm((BATCH, D_MODEL), 1.0),
        'ctx': nrm((BATCH, CTX_LEN, D_MODEL), 1.0),
        'c_ctx': nrm((D_MODEL,), 1.0),
        'mod_w': nrm((DEPTH, D_MODEL, 6 * D_MODEL), 0.5 * D_MODEL ** -0.5),
        'mod_b': nrm((DEPTH, 6 * D_MODEL), 0.02),
        'norm1_g': 1.0 + nrm((DEPTH, D_MODEL), 0.02),
        'norm2_g': 1.0 + nrm((DEPTH, D_MODEL), 0.02),
        'hyb_in_w': nrm((n_even, D_MODEL, HYB_IN), D_MODEL ** -0.5),
        'hyb_out_w': nrm((n_even, HYB_OUT, D_MODEL), HYB_OUT ** -0.5),
        's5_a_re': -0.5 + nrm((n_even, 2, S5_GROUPS, S5_STATE), 0.01),
        's5_a_im': math.pi * s5_n + nrm((n_even, 2, S5_GROUPS, S5_STATE), 0.01),
        's5_log_dt': jax.random.uniform(next(ks), (n_even, 2, S5_GROUPS), f32,
                                        minval=math.log(S5_DT_MIN), maxval=math.log(S5_DT_MAX)),
        's5_b_re': nrm((n_even, 2, S5_GROUPS, S5_STATE, S5_GROUP_CH), (2 * S5_GROUP_CH) ** -0.5),
        's5_b_im': nrm((n_even, 2, S5_GROUPS, S5_STATE, S5_GROUP_CH), (2 * S5_GROUP_CH) ** -0.5),
        's5_c_re': nrm((n_even, 2, S5_GROUPS, S5_GROUP_CH, S5_STATE), 0.5),
        's5_c_im': nrm((n_even, 2, S5_GROUPS, S5_GROUP_CH, S5_STATE), 0.5),
        's5_d': nrm((n_even, S5_WIDTH), 1.0),
        's5_glu_w': nrm((n_even, S5_WIDTH, S5_WIDTH), S5_WIDTH ** -0.5),
        's5_glu_b': nrm((n_even, S5_WIDTH), 0.02),
        'gm_ln_g': 1.0 + nrm((n_even, GM_WIDTH), 0.02),
        'gm_ln_b': nrm((n_even, GM_WIDTH), 0.02),
        'gm_ws': nrm((n_even, GM_HEADS, GM_CHUNK, GM_CHUNK), GM_CHUNK ** -0.5),
        'gm_bs': 1.0 + nrm((n_even, GM_HEADS, GM_CHUNK), 0.1),
        'mla_in_w': nrm((n_odd, D_MODEL, MLA_IN), D_MODEL ** -0.5),
        'mla_q_norm_g': 1.0 + nrm((n_odd, MLA_Q_LORA), 0.02),
        'mla_kv_norm_g': 1.0 + nrm((n_odd, MLA_KV_LORA), 0.02),
        'mla_wq_b': nrm((n_odd, MLA_Q_LORA, MLA_HEADS * (MLA_NOPE + MLA_ROPE)), MLA_Q_LORA ** -0.5),
        'mla_wkv_b': nrm((n_odd, MLA_KV_LORA, MLA_HEADS * (MLA_NOPE + MLA_V)), MLA_KV_LORA ** -0.5),
        'mla_qn_g': 1.0 + nrm((n_odd, MLA_NOPE), 0.02),
        'mla_qr_g': 1.0 + nrm((n_odd, MLA_ROPE), 0.02),
        'mla_kn_g': 1.0 + nrm((n_odd, MLA_NOPE), 0.02),
        'mla_kr_g': 1.0 + nrm((n_odd, MLA_ROPE), 0.02),
        'mla_out_w': nrm((n_odd, MLA_HEADS * MLA_V, D_MODEL), (MLA_HEADS * MLA_V) ** -0.5),
        'router_w': nrm((D_MODEL, N_EXPERTS), D_MODEL ** -0.5),
        'router_bias': nrm((N_EXPERTS,), 0.01),
        'moe_w1': nrm((DEPTH, N_EXPERTS, D_MODEL, EXPERT_FF), D_MODEL ** -0.5),
        'moe_w3': nrm((DEPTH, N_EXPERTS, D_MODEL, EXPERT_FF), D_MODEL ** -0.5),
        'moe_w2': nrm((DEPTH, N_EXPERTS, EXPERT_FF, D_MODEL), EXPERT_FF ** -0.5),
    }


def reference(x, c, ctx, c_ctx, mod_w, mod_b, norm1_g, norm2_g, hyb_in_w, hyb_out_w,
              s5_a_re, s5_a_im, s5_log_dt, s5_b_re, s5_b_im, s5_c_re, s5_c_im, s5_d, s5_glu_w, s5_glu_b,
              gm_ln_g, gm_ln_b, gm_ws, gm_bs,
              mla_in_w, mla_q_norm_g, mla_kv_norm_g, mla_wq_b, mla_wkv_b, mla_qn_g, mla_qr_g, mla_kn_g, mla_kr_g,
              mla_out_w, router_w, router_bias, moe_w1, moe_w3, moe_w2):
    L = x.shape[1]
    rows = L // GRID_W
    pos_row = jnp.repeat(jnp.arange(rows, dtype=jnp.float32), GRID_W)
    pos_col = jnp.tile(jnp.arange(GRID_W, dtype=jnp.float32), rows)
    rope_pos = (pos_row, pos_col)
    c_lat = c[:, None, :]
    for layer in range(DEPTH):
        need_ctx = layer < DEPTH - 1
        i = layer // 2
        sh1, sc1, g1, sh2, sc2, g2 = adaln(c_lat, mod_w[layer], mod_b[layer], 6)
        cm = adaln(c_ctx, mod_w[layer], mod_b[layer], 6 if need_ctx else 3)
        h_lat = modulate(x, norm1_g[layer], sh1, sc1)
        h_ctx = modulate(ctx, norm1_g[layer], cm[0], cm[1])
        if layer % 2 == 0:
            y_ctx, y_lat = hybrid_mixer(h_ctx, h_lat, hyb_in_w[i], hyb_out_w[i],
                                        s5_a_re[i], s5_a_im[i], s5_log_dt[i], s5_b_re[i], s5_b_im[i],
                                        s5_c_re[i], s5_c_im[i], s5_d[i], s5_glu_w[i], s5_glu_b[i],
                                        gm_ln_g[i], gm_ln_b[i], gm_ws[i], gm_bs[i], need_ctx)
        else:
            y_ctx, y_lat = mla_mixer(h_ctx, h_lat, mla_in_w[i], mla_q_norm_g[i], mla_kv_norm_g[i],
                                     mla_wq_b[i], mla_wkv_b[i], mla_qn_g[i], mla_qr_g[i], mla_kn_g[i],
                                     mla_kr_g[i], mla_out_w[i], rope_pos, need_ctx)
        x = x + g1 * y_lat
        x = x + g2 * moe(modulate(x, norm2_g[layer], sh2, sc2), router_w, router_bias,
                         moe_w1[layer], moe_w3[layer], moe_w2[layer])
        if need_ctx:
            ctx = ctx + cm[2] * y_ctx
            ctx = ctx + cm[5] * moe(modulate(ctx, norm2_g[layer], cm[3], cm[4]), router_w, router_bias,
                                    moe_w1[layer], moe_w3[layer], moe_w2[layer])
    return x
```

```python
import functools
import math

import jax
import jax.numpy as jnp
from jax import lax
from jax.experimental import pallas as pl
from jax.experimental.pallas import tpu as pltpu

F32 = jnp.float32
BF16 = jnp.bfloat16
HIGHEST = lax.Precision.HIGHEST

NORM_EPS = 1e-6
ROPE_BASE = 10000.0
GRID_W = 64

S5_GROUP_CH = 16
S5_STATE = 64
S5_BATCH_TILE = 8
S5_CHUNK = 128
S5_COLS = 512

GM_HEADS = 8
GM_CHUNK = 128

MLA_NOPE = 128
MLA_ROPE = 64
MLA_V = 128
MLA_QK = 256

N_EXPERTS = 16
N_GROUPS = 4
GROUP_SIZE = 4

ROW_TILE = 256
MOE_TILE = 768
VMEM_LIMIT = 56 * 1024 * 1024


def _cparams(*sem):
    return pltpu.CompilerParams(dimension_semantics=sem, vmem_limit_bytes=VMEM_LIMIT)


def _sigmoid(x):
    return 1.0 / (1.0 + jnp.exp(-x))


def _gelu_tanh(x):
    c = math.sqrt(2.0 / math.pi)
    return 0.5 * x * (1.0 + jnp.tanh(c * (x + 0.044715 * (x * x * x))))


def _rms(x, n=None):
    n = x.shape[-1] if n is None else n
    return x * lax.rsqrt(jnp.sum(x * x, axis=-1, keepdims=True) * (1.0 / n) + NORM_EPS)


def _modulate(x, g, shift, scale):
    return (_rms(x) * g) * (1.0 + scale) + shift


def _bdot(a, b):
    return jnp.dot(a, b, preferred_element_type=F32)


def _adaln_kernel(c_ref, w_ref, b_ref, o_ref):
    c = c_ref[...]
    s = c * _sigmoid(c)
    o_ref[0] = jnp.dot(s, w_ref[0], precision=HIGHEST, preferred_element_type=F32) + b_ref[0]


def _adaln(cond, mod_w, mod_b):
    depth, d, n = mod_w.shape
    bp = cond.shape[0]
    tn = n // 4
    return pl.pallas_call(
        _adaln_kernel,
        grid=(depth, n // tn),
        in_specs=[
            pl.BlockSpec((bp, d), lambda l, j: (0, 0)),
            pl.BlockSpec((1, d, tn), lambda l, j: (l, 0, j)),
            pl.BlockSpec((1, 1, tn), lambda l, j: (l, 0, j)),
        ],
        out_specs=pl.BlockSpec((1, bp, tn), lambda l, j: (l, 0, j)),
        out_shape=jax.ShapeDtypeStruct((depth, bp, n), F32),
        compiler_params=_cparams("parallel", "parallel"),
        name="adaln",
    )(cond, mod_w, mod_b.reshape(depth, 1, n))


def _mod_spec(d):
    return pl.BlockSpec((1, 2, 6, d), lambda b, i: (b, 0, 0, 0))


def _row_spec(tm, n):
    return pl.BlockSpec((1, tm, n), lambda b, i: (b, i, 0))


def _full_spec(shape):
    nd = len(shape)
    return pl.BlockSpec(shape, lambda b, i: (0,) * nd)


def _hyb_in_kernel(x_ref, m_ref, g_ref, w_ref, u_ref, zg_ref, *, nctx, s5w):
    seg = (pl.program_id(1) >= nctx).astype(jnp.int32)
    h = _modulate(x_ref[0], g_ref[...], m_ref[0, seg, 0:1, :], m_ref[0, seg, 1:2, :])
    z = _bdot(h.astype(BF16), w_ref[...])
    u_ref[0] = z[:, :s5w].astype(BF16)
    zg_ref[0] = z[:, s5w:].astype(BF16)


def _hyb_in(x, modsel, g, w, lc):
    b, s, d = x.shape
    n = w.shape[1]
    s5w = d // 2
    tm = ROW_TILE
    return pl.pallas_call(
        functools.partial(_hyb_in_kernel, nctx=lc // tm, s5w=s5w),
        grid=(b, s // tm),
        in_specs=[_row_spec(tm, d), _mod_spec(d), _full_spec((1, d)), _full_spec((d, n))],
        out_specs=[_row_spec(tm, s5w), _row_spec(tm, n - s5w)],
        out_shape=[jax.ShapeDtypeStruct((b, s, s5w), BF16), jax.ShapeDtypeStruct((b, s, n - s5w), BF16)],
        compiler_params=_cparams("parallel", "parallel"),
        name="hyb_in",
    )(x, modsel, g.reshape(1, d), w)


def _s5_disc_kernel(are_ref, aim_ref, ldt_ref, bre_ref, bim_ref, abre_ref, abim_ref, bbre_ref, bbim_ref):
    a_re = are_ref[0]
    a_im = aim_ref[0]
    dt = jnp.exp(ldt_ref[0])
    mag = jnp.exp(dt * a_re)
    ab_re = mag * jnp.cos(dt * a_im)
    ab_im = mag * jnp.sin(dt * a_im)
    nr = ab_re - 1.0
    ni = ab_im
    den = a_re * a_re + a_im * a_im
    f_re = (nr * a_re + ni * a_im) / den
    f_im = (ni * a_re - nr * a_im) / den
    abre_ref[0] = ab_re
    abim_ref[0] = ab_im
    b_re = bre_ref[0]
    b_im = bim_ref[0]
    bbre_ref[0] = f_re[None] * b_re - f_im[None] * b_im
    bbim_ref[0] = f_re[None] * b_im + f_im[None] * b_re


def _s5_discretize(a_re, a_im, log_dt, b_re, b_im):
    nd, g, n = a_re.shape
    p = b_re.shape[-1]
    bt_re = jnp.transpose(b_re, (0, 3, 1, 2))
    bt_im = jnp.transpose(b_im, (0, 3, 1, 2))
    s3 = pl.BlockSpec((1, g, n), lambda i: (i, 0, 0))
    s4 = pl.BlockSpec((1, p, g, n), lambda i: (i, 0, 0, 0))
    return pl.pallas_call(
        _s5_disc_kernel,
        grid=(nd,),
        in_specs=[s3, s3, pl.BlockSpec((1, g, 1), lambda i: (i, 0, 0)), s4, s4],
        out_specs=[s3, s3, s4, s4],
        out_shape=[jax.ShapeDtypeStruct((nd, g, n), F32)] * 2 + [jax.ShapeDtypeStruct((nd, p, g, n), F32)] * 2,
        compiler_params=_cparams("parallel"),
        name="s5_disc",
    )(a_re, a_im, log_dt.reshape(nd, g, 1), bt_re, bt_im)


def _s5_matrices(ab_re, ab_im, bb_re, bb_im, c_re, c_im):
    p, g, n = bb_re.shape
    gh = g // 2
    eye = jnp.eye(gh, dtype=F32)

    def in_half(bb, hh):
        blk = jnp.transpose(bb[:, hh * gh:(hh + 1) * gh], (1, 0, 2))
        return jnp.einsum("gpn,gk->gpkn", blk, eye).reshape(gh * p, gh * n)

    def out_half(c, hh):
        blk = c[hh * gh:(hh + 1) * gh]
        return jnp.einsum("gpn,gk->gnkp", blk, eye).reshape(gh * n, gh * p)

    bm = jnp.stack([jnp.concatenate([in_half(bb_re, hh), in_half(bb_im, hh)], axis=1) for hh in range(2)])
    cre = jnp.stack([out_half(c_re, hh) for hh in range(2)])
    cim = jnp.stack([out_half(c_im, hh) for hh in range(2)])
    return (bm.astype(BF16), ab_re.reshape(1, g * n), ab_im.reshape(1, g * n), cre.astype(BF16), cim.astype(BF16))


def _s5_scan_kernel(u_ref, bm_ref, are_ref, aim_ref, cre_ref, cim_ref, y_ref, bure, buim, hre, him, *, reverse):
    bt, t_len, w = u_ref.shape
    ns = hre.shape[1]
    half = ns // 2

    @pl.when(pl.program_id(1) == 0)
    def _():
        hre[...] = jnp.zeros_like(hre)
        him[...] = jnp.zeros_like(him)

    u_tb = pltpu.einshape("btc->tbc", u_ref[...].astype(F32)).reshape(t_len * bt, w).astype(BF16)
    for hh in range(2):
        bu = _bdot(u_tb[:, hh * (w // 2):(hh + 1) * (w // 2)], bm_ref[hh])
        bure[:, hh * half:(hh + 1) * half] = bu[:, :half]
        buim[:, hh * half:(hh + 1) * half] = bu[:, half:]

    for c0 in range(0, ns, S5_COLS):
        cs = slice(c0, c0 + S5_COLS)
        a_r = jnp.broadcast_to(are_ref[:, cs], (bt, S5_COLS))
        a_i = jnp.broadcast_to(aim_ref[:, cs], (bt, S5_COLS))

        def body(t, carry, cs=cs, a_r=a_r, a_i=a_i):
            h_r, h_i = carry
            tt = (t_len - 1 - t) if reverse else t
            rows = pl.ds(pl.multiple_of(tt * bt, bt), bt)
            n_r = a_r * h_r - a_i * h_i + bure[rows, cs]
            n_i = a_r * h_i + a_i * h_r + buim[rows, cs]
            bure[rows, cs] = n_r
            buim[rows, cs] = n_i
            return n_r, n_i

        h_r, h_i = lax.fori_loop(0, t_len, body, (hre[:, cs], him[:, cs]), unroll=4)
        hre[:, cs] = h_r
        him[:, cs] = h_i

    ys = []
    for hh in range(2):
        x_r = bure[:, hh * half:(hh + 1) * half].astype(BF16)
        x_i = buim[:, hh * half:(hh + 1) * half].astype(BF16)
        ys.append(_bdot(x_r, cre_ref[hh]) - _bdot(x_i, cim_ref[hh]))
    y_tb = jnp.concatenate(ys, axis=1).reshape(t_len, bt, w)
    y_ref[...] = pltpu.einshape("tbc->btc", y_tb).astype(BF16)


def _s5_scan(u, mats, lc, reverse):
    bm, a_r, a_i, cre, cim = mats
    b, s, w = u.shape
    bt, t_len = S5_BATCH_TILE, S5_CHUNK
    nch, nctx = s // t_len, lc // t_len
    ns = a_r.shape[1]

    def chunk(k):
        if not reverse:
            return k
        return jnp.where(k < nctx, nctx - 1 - k, nch - 1 - (k - nctx))

    blk = pl.BlockSpec((bt, t_len, w), lambda g, k: (g, chunk(k), 0))
    return pl.pallas_call(
        functools.partial(_s5_scan_kernel, reverse=reverse),
        grid=(b // bt, nch),
        in_specs=[blk, _full_spec(bm.shape), _full_spec(a_r.shape), _full_spec(a_i.shape),
                  _full_spec(cre.shape), _full_spec(cim.shape)],
        out_specs=blk,
        out_shape=jax.ShapeDtypeStruct((b, s, w), BF16),
        scratch_shapes=[pltpu.VMEM((t_len * bt, ns), F32), pltpu.VMEM((t_len * bt, ns), F32),
                        pltpu.VMEM((bt, ns), F32), pltpu.VMEM((bt, ns), F32)],
        compiler_params=_cparams("parallel", "arbitrary"),
        name="s5_bwd" if reverse else "s5_fwd",
    )(u, bm, a_r, a_i, cre, cim)


def _hyb_mid_kernel(x_ref, m_ref, u_ref, yf_ref, yb_ref, zg_ref, d_ref, gw_ref, gb_ref, lg_ref, lb_ref,
                    ws_ref, bsx_ref, ow_ref, o_ref, *, nctx):
    seg = (pl.program_id(1) >= nctx).astype(jnp.int32)
    tm = x_ref.shape[1]
    w = u_ref.shape[2]
    y = u_ref[0].astype(F32) * d_ref[...] + yf_ref[0].astype(F32) + yb_ref[0].astype(F32)
    sg = _gelu_tanh(y)
    s5o = sg * _sigmoid(_bdot(sg.astype(BF16), gw_ref[...]) + gb_ref[...])
    zg = _gelu_tanh(zg_ref[0].astype(F32))
    gu = zg[:, :w]
    v = zg[:, w:]
    mu = jnp.mean(v, axis=-1, keepdims=True)
    vc = v - mu
    var = jnp.mean(vc * vc, axis=-1, keepdims=True)
    vn = (vc * lax.rsqrt(var + NORM_EPS) * lg_ref[...] + lb_ref[...]).astype(BF16)
    lane = lax.broadcasted_iota(jnp.int32, (GM_CHUNK, 128), 1)
    low = lane < (w // GM_HEADS)
    chunks = []
    for c in range(tm // GM_CHUNK):
        vch = vn[c * GM_CHUNK:(c + 1) * GM_CHUNK]
        cols = []
        for kp in range(GM_HEADS // 2):
            vs = vch[:, kp * 128:(kp + 1) * 128]
            cols.append(jnp.where(low, _bdot(ws_ref[2 * kp], vs), _bdot(ws_ref[2 * kp + 1], vs)))
        chunks.append(jnp.concatenate(cols, axis=1) + bsx_ref[...])
    gm = gu * jnp.concatenate(chunks, axis=0)
    ycat = jnp.concatenate([s5o, gm], axis=1).astype(BF16)
    o_ref[0] = x_ref[0] + m_ref[0, seg, 2:3, :] * _bdot(ycat, ow_ref[...])


def _hyb_mid(x, modsel, u, yf, yb, zg, d, glu_w, glu_b, ln_g, ln_b, ws, bsx, out_w, lc):
    b, s, dm = x.shape
    w = u.shape[2]
    tm = ROW_TILE
    vec = _full_spec((1, w))
    return pl.pallas_call(
        functools.partial(_hyb_mid_kernel, nctx=lc // tm),
        grid=(b, s // tm),
        in_specs=[_row_spec(tm, dm), _mod_spec(dm), _row_spec(tm, w), _row_spec(tm, w), _row_spec(tm, w),
                  _row_spec(tm, 2 * w), vec, _full_spec((w, w)), vec, vec, vec,
                  _full_spec(ws.shape), _full_spec(bsx.shape), _full_spec(out_w.shape)],
        out_specs=_row_spec(tm, dm),
        out_shape=jax.ShapeDtypeStruct((b, s, dm), F32),
        compiler_params=_cparams("parallel", "parallel"),
        name="hyb_mid",
    )(x, modsel, u, yf, yb, zg, d.reshape(1, w), glu_w, glu_b.reshape(1, w), ln_g.reshape(1, w),
      ln_b.reshape(1, w), ws, bsx, out_w)


def _mla_in_kernel(x_ref, m_ref, g_ref, win_ref, qng_ref, kvng_ref, wq_ref, wkv_ref, qn_ref, qr_ref, kn_ref,
                   kr_ref, rot_ref, cos_ref, sin_ref, q_out, k_out, v_out, *, nctx, q_lora, kv_lora, heads, scale):
    seg = (pl.program_id(1) >= nctx).astype(jnp.int32)
    h = _modulate(x_ref[0], g_ref[...], m_ref[0, seg, 0:1, :], m_ref[0, seg, 1:2, :])
    z = _bdot(h.astype(BF16), win_ref[...])
    cq = z[:, :q_lora]
    ckv = z[:, q_lora:q_lora + kv_lora]
    kr = z[:, q_lora + kv_lora:]
    q = _bdot((_rms(cq) * qng_ref[...]).astype(BF16), wq_ref[...])
    kv = _bdot((_rms(ckv) * kvng_ref[...]).astype(BF16), wkv_ref[...])
    cos = cos_ref[...]
    sin = sin_ref[...]

    def rope(t):
        return t * cos + _bdot(t.astype(BF16), rot_ref[...]) * sin

    k_rope = rope(_rms(kr, MLA_ROPE) * kr_ref[...]).astype(BF16)
    for hd in range(heads):
        qh = q[:, hd * MLA_QK:(hd + 1) * MLA_QK]
        q_out[0, hd, :, :MLA_NOPE] = (_rms(qh[:, :MLA_NOPE]) * qn_ref[...] * scale).astype(BF16)
        q_out[0, hd, :, MLA_NOPE:] = (rope(_rms(qh[:, MLA_NOPE:], MLA_ROPE) * qr_ref[...]) * scale).astype(BF16)
        kvh = kv[:, hd * (MLA_NOPE + MLA_V):(hd + 1) * (MLA_NOPE + MLA_V)]
        k_out[0, hd, :, :MLA_NOPE] = (_rms(kvh[:, :MLA_NOPE]) * kn_ref[...]).astype(BF16)
        k_out[0, hd, :, MLA_NOPE:] = k_rope
        v_out[0, hd] = kvh[:, MLA_NOPE:].astype(BF16)


def _mla_in(x, modsel, g, win, qng, kvng, wq, wkv, qn_g, qr_g, kn_g, kr_g, rot, cos, sin, lc, heads):
    b, s, d = x.shape
    tm = ROW_TILE
    q_lora, kv_lora = qng.shape[0], kvng.shape[0]
    scale = (MLA_NOPE + MLA_ROPE) ** -0.5
    pad = jnp.zeros((MLA_QK - MLA_NOPE - MLA_ROPE,), F32)
    vec = lambda a: _full_spec((1, a.shape[-1]))
    args = [g.reshape(1, d), win, qng.reshape(1, -1), kvng.reshape(1, -1), wq, wkv, qn_g.reshape(1, -1),
            jnp.concatenate([qr_g, pad]).reshape(1, -1), kn_g.reshape(1, -1),
            jnp.concatenate([kr_g, pad]).reshape(1, -1), rot]
    tab = pl.BlockSpec((tm, 128), lambda bb, i: (i, 0))
    head_spec = lambda n: pl.BlockSpec((1, heads, tm, n), lambda bb, i: (bb, 0, i, 0))
    return pl.pallas_call(
        functools.partial(_mla_in_kernel, nctx=lc // tm, q_lora=q_lora, kv_lora=kv_lora, heads=heads, scale=scale),
        grid=(b, s // tm),
        in_specs=[_row_spec(tm, d), _mod_spec(d)] + [_full_spec(a.shape) for a in args] + [tab, tab],
        out_specs=[head_spec(MLA_QK), head_spec(MLA_QK), head_spec(MLA_V)],
        out_shape=[jax.ShapeDtypeStruct((b, heads, s, MLA_QK), BF16), jax.ShapeDtypeStruct((b, heads, s, MLA_QK), BF16),
                   jax.ShapeDtypeStruct((b, heads, s, MLA_V), BF16)],
        compiler_params=_cparams("parallel", "parallel"),
        name="mla_in",
    )(x, modsel, *args, cos, sin)


def _attn_kernel(q_ref, k_ref, v_ref, o_ref, *, lc, nq_ctx):
    i = pl.program_id(2)
    s_len = k_ref.shape[2]

    def run(n):
        q = q_ref[0, 0]
        k = k_ref[0, 0, :n, :]
        v = v_ref[0, 0, :n, :]
        s = lax.dot_general(q, k, (((1,), (1,)), ((), ())), preferred_element_type=F32)
        p = jnp.exp(s - jnp.max(s, axis=-1, keepdims=True))
        l = jnp.sum(p, axis=-1, keepdims=True)
        o_ref[0] = (_bdot(p.astype(BF16), v) / l).astype(BF16)

    pl.when(i < nq_ctx)(lambda: run(lc))
    pl.when(i >= nq_ctx)(lambda: run(s_len))


def _attention(q, k, v, lc):
    b, heads, s, dq = q.shape
    dv = v.shape[3]
    tq = ROW_TILE
    kv_spec = lambda n: pl.BlockSpec((1, 1, s, n), lambda bb, h, i: (bb, h, 0, 0))
    return pl.pallas_call(
        functools.partial(_attn_kernel, lc=lc, nq_ctx=lc // tq),
        grid=(b, heads, s // tq),
        in_specs=[pl.BlockSpec((1, 1, tq, dq), lambda bb, h, i: (bb, h, i, 0)), kv_spec(dq), kv_spec(dv)],
        out_specs=pl.BlockSpec((1, tq, dv), lambda bb, h, i: (bb, i, h)),
        out_shape=jax.ShapeDtypeStruct((b, s, heads * dv), BF16),
        compiler_params=_cparams("parallel", "parallel", "parallel"),
        name="mla_attn",
    )(q, k, v)


def _mla_out_kernel(x_ref, m_ref, o_ref, w_ref, out_ref, *, nctx):
    seg = (pl.program_id(1) >= nctx).astype(jnp.int32)
    out_ref[0] = x_ref[0] + m_ref[0, seg, 2:3, :] * _bdot(o_ref[0], w_ref[...])


def _mla_out(x, modsel, o, w, lc):
    b, s, d = x.shape
    tm = ROW_TILE
    return pl.pallas_call(
        functools.partial(_mla_out_kernel, nctx=lc // tm),
        grid=(b, s // tm),
        in_specs=[_row_spec(tm, d), _mod_spec(d), _row_spec(tm, o.shape[2]), _full_spec(w.shape)],
        out_specs=_row_spec(tm, d),
        out_shape=jax.ShapeDtypeStruct((b, s, d), F32),
        compiler_params=_cparams("parallel", "parallel"),
        name="mla_out",
    )(x, modsel, o, w)


def _router_kernel(x_ref, m_ref, g_ref, rwt_ref, rb_ref, h_ref, gates_ref, *, nctx):
    seg = (pl.program_id(1) >= nctx).astype(jnp.int32)
    tm = x_ref.shape[1]
    h = _modulate(x_ref[0], g_ref[...], m_ref[0, seg, 3:4, :], m_ref[0, seg, 4:5, :])
    h_ref[0] = h.astype(BF16)
    logits = lax.dot_general(rwt_ref[...], h, (((1,), (1,)), ((), ())), precision=HIGHEST,
                             preferred_element_type=F32)
    scores = _sigmoid(logits)
    sel = scores + rb_ref[...]
    sel_r = [sel[e:e + 1, :] for e in range(N_EXPERTS)]
    sc_r = [scores[e:e + 1, :] for e in range(N_EXPERTS)]

    grp = []
    for g in range(N_GROUPS):
        a, b, c, d = sel_r[GROUP_SIZE * g:GROUP_SIZE * (g + 1)]
        hi1, lo1, hi2, lo2 = jnp.maximum(a, b), jnp.minimum(a, b), jnp.maximum(c, d), jnp.minimum(c, d)
        grp.append(jnp.maximum(hi1, hi2) + jnp.maximum(jnp.minimum(hi1, hi2), jnp.maximum(lo1, lo2)))
    best = jnp.zeros((1, tm), jnp.int32)
    cur = grp[0]
    for g in range(1, N_GROUPS):
        upd = grp[g] > cur
        best = jnp.where(upd, g, best)
        cur = jnp.where(upd, grp[g], cur)

    def in_group(rows, j):
        out = rows[j]
        for g in range(1, N_GROUPS):
            out = jnp.where(best == g, rows[GROUP_SIZE * g + j], out)
        return out

    v = [in_group(sel_r, j) for j in range(GROUP_SIZE)]
    sc = [in_group(sc_r, j) for j in range(GROUP_SIZE)]

    def first_argmax(vals):
        idx = jnp.zeros((1, tm), jnp.int32)
        top = vals[0]
        for j in range(1, GROUP_SIZE):
            upd = vals[j] > top
            idx = jnp.where(upd, j, idx)
            top = jnp.where(upd, vals[j], top)
        return idx

    i1 = first_argmax(v)
    i2 = first_argmax([jnp.where(i1 == j, -jnp.inf, v[j]) for j in range(GROUP_SIZE)])
    w1 = sum(jnp.where(i1 == j, sc[j], 0.0) for j in range(GROUP_SIZE))
    w2 = sum(jnp.where(i2 == j, sc[j], 0.0) for j in range(GROUP_SIZE))
    den = w1 + w2
    gj = [(jnp.where(i1 == j, w1, 0.0) + jnp.where(i2 == j, w2, 0.0)) / den for j in range(GROUP_SIZE)]

    row = lax.broadcasted_iota(jnp.int32, (N_EXPERTS, tm), 0)
    gates_t = jnp.zeros((N_EXPERTS, tm), F32)
    for e in range(N_EXPERTS):
        ge = jnp.where(best == e // GROUP_SIZE, gj[e % GROUP_SIZE], 0.0)
        gates_t = jnp.where(row == e, jnp.broadcast_to(ge, (N_EXPERTS, tm)), gates_t)
    padded = jnp.concatenate([gates_t, jnp.zeros((128 - N_EXPERTS, tm), F32)], axis=0)
    gates_ref[0] = padded.T[:, :N_EXPERTS]


def _router(x, modsel, g, router_w, router_bias, lc):
    b, s, d = x.shape
    tm = ROW_TILE
    return pl.pallas_call(
        functools.partial(_router_kernel, nctx=lc // tm),
        grid=(b, s // tm),
        in_specs=[_row_spec(tm, d), _mod_spec(d), _full_spec((1, d)), _full_spec((N_EXPERTS, d)),
                  _full_spec((N_EXPERTS, 1))],
        out_specs=[_row_spec(tm, d), _row_spec(tm, N_EXPERTS)],
        out_shape=[jax.ShapeDtypeStruct((b, s, d), BF16), jax.ShapeDtypeStruct((b, s, N_EXPERTS), F32)],
        compiler_params=_cparams("parallel", "parallel"),
        name="moe_router",
    )(x, modsel, g.reshape(1, d), router_w.T, router_bias.reshape(N_EXPERTS, 1))


def _moe_dense_kernel(x_ref, h_ref, gates_ref, m_ref, w1_ref, w3_ref, w2_ref, o_ref, acc_ref, *, lc, nt):
    e = pl.program_id(1)
    tm = x_ref.shape[1]

    @pl.when(e == 0)
    def _():
        acc_ref[...] = jnp.zeros_like(acc_ref)

    h = h_ref[0]
    a = _bdot(h, w1_ref[0])
    b = _bdot(h, w3_ref[0])
    g = gates_ref[0]
    lane = lax.broadcasted_iota(jnp.int32, g.shape, 1)
    ge = jnp.sum(jnp.where(lane == e, g, 0.0), axis=-1, keepdims=True)
    hid = (a * _sigmoid(a)) * b * ge
    acc_ref[...] += _bdot(hid.astype(BF16), w2_ref[0])

    @pl.when(e == pl.num_programs(1) - 1)
    def _():
        rows = (pl.program_id(0) % nt) * tm + lax.broadcasted_iota(jnp.int32, (tm, 1), 0)
        g2 = jnp.where(rows < lc, m_ref[0, 0, 5:6, :], m_ref[0, 1, 5:6, :])
        o_ref[0] = x_ref[0] + g2 * acc_ref[...]


def _moe_dense(x, h, gates, modsel, w1, w3, w2, lc):
    b, s, d = x.shape
    ne, _, f = w1.shape
    tm = MOE_TILE if s % MOE_TILE == 0 else ROW_TILE
    nt = s // tm
    row = lambda n: pl.BlockSpec((1, tm, n), lambda t, e: (t // nt, t % nt, 0))
    return pl.pallas_call(
        functools.partial(_moe_dense_kernel, lc=lc, nt=nt),
        grid=(b * nt, ne),
        in_specs=[row(d), row(d), row(ne), pl.BlockSpec((1, 2, 6, d), lambda t, e: (t // nt, 0, 0, 0)),
                  pl.BlockSpec((1, d, f), lambda t, e: (e, 0, 0)), pl.BlockSpec((1, d, f), lambda t, e: (e, 0, 0)),
                  pl.BlockSpec((1, f, d), lambda t, e: (e, 0, 0))],
        out_specs=row(d),
        out_shape=jax.ShapeDtypeStruct((b, s, d), F32),
        scratch_shapes=[pltpu.VMEM((tm, d), F32)],
        compiler_params=_cparams("parallel", "arbitrary"),
        name="moe_experts",
    )(x, h, gates, modsel, w1, w3, w2)


def _rope_tables(lc, l):
    half = MLA_ROPE // 4
    freqs = ROPE_BASE ** (-jnp.arange(half, dtype=F32) / half)
    pos = jnp.arange(l)
    pos_row = (pos // GRID_W).astype(F32)
    pos_col = (pos % GRID_W).astype(F32)
    ang = jnp.concatenate([pos_row[:, None] * freqs, pos_row[:, None] * freqs,
                           pos_col[:, None] * freqs, pos_col[:, None] * freqs], axis=1)
    pad = jnp.zeros((l, 128 - MLA_ROPE), F32)
    cos = jnp.concatenate([jnp.concatenate([jnp.ones((lc, MLA_ROPE), F32), jnp.zeros((lc, 128 - MLA_ROPE), F32)], axis=1),
                           jnp.concatenate([jnp.cos(ang), pad], axis=1)], axis=0)
    sin = jnp.concatenate([jnp.zeros((lc, 128), F32), jnp.concatenate([jnp.sin(ang), pad], axis=1)], axis=0)
    r = jnp.zeros((128, 128), F32)
    idx = jnp.arange(half)
    for base in (0, MLA_ROPE // 2):
        r = r.at[base + half + idx, base + idx].set(-1.0)
        r = r.at[base + idx, base + half + idx].set(1.0)
    return cos, sin, r.astype(BF16)


def _pad_heads(w, heads):
    k = w.shape[0]
    w = w.reshape(k, heads, MLA_NOPE + MLA_ROPE)
    w = jnp.concatenate([w, jnp.zeros((k, heads, MLA_QK - MLA_NOPE - MLA_ROPE), w.dtype)], axis=2)
    return w.reshape(k, heads * MLA_QK)


def kernel(x, c, ctx, c_ctx, mod_w, mod_b, norm1_g, norm2_g, hyb_in_w, hyb_out_w, s5_a_re, s5_a_im, s5_log_dt, s5_b_re, s5_b_im, s5_c_re, s5_c_im, s5_d, s5_glu_w, s5_glu_b, gm_ln_g, gm_ln_b, gm_ws, gm_bs, mla_in_w, mla_q_norm_g, mla_kv_norm_g, mla_wq_b, mla_wkv_b, mla_qn_g, mla_qr_g, mla_kn_g, mla_kr_g, mla_out_w, router_w, router_bias, moe_w1, moe_w3, moe_w2):
    b, l, d = x.shape
    lc = ctx.shape[1]
    depth = mod_w.shape[0]
    heads = mla_out_w.shape[1] // MLA_V
    assert lc % ROW_TILE == 0 and l % ROW_TILE == 0 and b % S5_BATCH_TILE == 0

    xs = jnp.concatenate([ctx, x], axis=1)

    bp = -(-(b + 1) // 8) * 8
    cond = jnp.concatenate([c, c_ctx[None], jnp.zeros((bp - b - 1, d), F32)], axis=0)
    mods = _adaln(cond, mod_w, mod_b)

    cos, sin, rot = _rope_tables(lc, l)

    for layer in range(depth):
        i = layer // 2
        m = mods[layer].reshape(bp, 6, d)
        modsel = jnp.stack([jnp.broadcast_to(m[b][None], (b, 6, d)), m[:b]], axis=1)

        if layer % 2 == 0:
            u, zg = _hyb_in(xs, modsel, norm1_g[layer], hyb_in_w[i].astype(BF16), lc)
            ab_re, ab_im, bb_re, bb_im = _s5_discretize(s5_a_re[i], s5_a_im[i], s5_log_dt[i], s5_b_re[i], s5_b_im[i])
            ys = []
            for dr in range(2):
                mats = _s5_matrices(ab_re[dr], ab_im[dr], bb_re[dr], bb_im[dr], s5_c_re[i, dr], s5_c_im[i, dr])
                ys.append(_s5_scan(u, mats, lc, reverse=bool(dr)))
            hd = (d // 2) // GM_HEADS
            bsx = jnp.repeat(gm_bs[i].T, hd, axis=1)
            xs = _hyb_mid(xs, modsel, u, ys[0], ys[1], zg, s5_d[i], s5_glu_w[i].astype(BF16), s5_glu_b[i],
                          gm_ln_g[i], gm_ln_b[i], gm_ws[i].astype(BF16), bsx, hyb_out_w[i].astype(BF16), lc)
        else:
            q_lora = mla_q_norm_g.shape[1]
            kv_lora = mla_kv_norm_g.shape[1]
            win = jnp.concatenate([mla_in_w[i], jnp.zeros((d, 128 - MLA_ROPE), F32)], axis=1).astype(BF16)
            wq = _pad_heads(mla_wq_b[i], heads).astype(BF16)
            q, k, v = _mla_in(xs, modsel, norm1_g[layer], win, mla_q_norm_g[i], mla_kv_norm_g[i], wq,
                              mla_wkv_b[i].astype(BF16), mla_qn_g[i], mla_qr_g[i], mla_kn_g[i], mla_kr_g[i],
                              rot, cos, sin, lc, heads)
            o = _attention(q, k, v, lc)
            xs = _mla_out(xs, modsel, o, mla_out_w[i].astype(BF16), lc)

        h, gates = _router(xs, modsel, norm2_g[layer], router_w, router_bias, lc)
        xs = _moe_dense(xs, h, gates, modsel, moe_w1[layer].astype(BF16), moe_w3[layer].astype(BF16),
                        moe_w2[layer].astype(BF16), lc)

    return xs[:, lc:, :]
```

```python
import functools
import math

import jax
import jax.numpy as jnp
from jax import lax
from jax.experimental import pallas as pl
from jax.experimental.pallas import tpu as pltpu

F32 = jnp.float32
BF16 = jnp.bfloat16
HIGHEST = lax.Precision.HIGHEST

NORM_EPS = 1e-6
ROPE_BASE = 10000.0
GRID_W = 64

S5_GROUP_CH = 16
S5_STATE = 64
S5_BATCH_TILE = 8
S5_CHUNK = 128
S5_COLS = 512

GM_HEADS = 8
GM_CHUNK = 128

MLA_NOPE = 128
MLA_ROPE = 64
MLA_V = 128
MLA_QK = 256

N_EXPERTS = 16
N_GROUPS = 4
GROUP_SIZE = 4

ROW_TILE = 256
ATTN_HEADS_PER_STEP = 2
N_CLASSES = 24
CLASS_ROWS = 32
EXP_TILE = 512
DISPATCH_TILE = 512
LANES = 128
SUBLANES = 8
VMEM_LIMIT = 56 * 1024 * 1024


def _cparams(*sem):
    return pltpu.CompilerParams(dimension_semantics=sem, vmem_limit_bytes=VMEM_LIMIT)


def _sigmoid(x):
    return 1.0 / (1.0 + jnp.exp(-x))


def _gelu_tanh(x):
    c = math.sqrt(2.0 / math.pi)
    return 0.5 * x * (1.0 + jnp.tanh(c * (x + 0.044715 * (x * x * x))))


def _rms(x, n=None):
    n = x.shape[-1] if n is None else n
    return x * lax.rsqrt(jnp.sum(x * x, axis=-1, keepdims=True) * (1.0 / n) + NORM_EPS)


def _modulate(x, g, shift, scale):
    return (_rms(x) * g) * (1.0 + scale) + shift


def _bdot(a, b):
    return jnp.dot(a, b, preferred_element_type=F32)


def _adaln_kernel(c_ref, w_ref, b_ref, o_ref):
    c = c_ref[...]
    s = c * _sigmoid(c)
    o_ref[0] = jnp.dot(s, w_ref[0], precision=HIGHEST, preferred_element_type=F32) + b_ref[0]


def _adaln(cond, mod_w, mod_b):
    depth, d, n = mod_w.shape
    bp = cond.shape[0]
    tn = n // 4
    return pl.pallas_call(
        _adaln_kernel,
        grid=(depth, n // tn),
        in_specs=[
            pl.BlockSpec((bp, d), lambda l, j: (0, 0)),
            pl.BlockSpec((1, d, tn), lambda l, j: (l, 0, j)),
            pl.BlockSpec((1, 1, tn), lambda l, j: (l, 0, j)),
        ],
        out_specs=pl.BlockSpec((1, bp, tn), lambda l, j: (l, 0, j)),
        out_shape=jax.ShapeDtypeStruct((depth, bp, n), F32),
        compiler_params=_cparams("parallel", "parallel"),
        name="adaln",
    )(cond, mod_w, mod_b.reshape(depth, 1, n))


def _mod_spec(d):
    return pl.BlockSpec((1, 2, 6, d), lambda b, i: (b, 0, 0, 0))


def _row_spec(tm, n):
    return pl.BlockSpec((1, tm, n), lambda b, i: (b, i, 0))


def _full_spec(shape):
    nd = len(shape)
    return pl.BlockSpec(shape, lambda b, i: (0,) * nd)


def _hyb_in_kernel(x_ref, m_ref, g_ref, w_ref, u_ref, zg_ref, *, nctx, s5w):
    seg = (pl.program_id(1) >= nctx).astype(jnp.int32)
    h = _modulate(x_ref[0], g_ref[...], m_ref[0, seg, 0:1, :], m_ref[0, seg, 1:2, :])
    z = _bdot(h.astype(BF16), w_ref[...])
    u_ref[0] = z[:, :s5w].astype(BF16)
    zg_ref[0] = z[:, s5w:].astype(BF16)


def _hyb_in(x, modsel, g, w, lc):
    b, s, d = x.shape
    n = w.shape[1]
    s5w = d // 2
    tm = ROW_TILE
    return pl.pallas_call(
        functools.partial(_hyb_in_kernel, nctx=lc // tm, s5w=s5w),
        grid=(b, s // tm),
        in_specs=[_row_spec(tm, d), _mod_spec(d), _full_spec((1, d)), _full_spec((d, n))],
        out_specs=[_row_spec(tm, s5w), _row_spec(tm, n - s5w)],
        out_shape=[jax.ShapeDtypeStruct((b, s, s5w), BF16), jax.ShapeDtypeStruct((b, s, n - s5w), BF16)],
        compiler_params=_cparams("parallel", "parallel"),
        name="hyb_in",
    )(x, modsel, g.reshape(1, d), w)


def _s5_disc_kernel(are_ref, aim_ref, ldt_ref, bre_ref, bim_ref, abre_ref, abim_ref, bbre_ref, bbim_ref):
    a_re = are_ref[0]
    a_im = aim_ref[0]
    dt = jnp.exp(ldt_ref[0])
    mag = jnp.exp(dt * a_re)
    ab_re = mag * jnp.cos(dt * a_im)
    ab_im = mag * jnp.sin(dt * a_im)
    nr = ab_re - 1.0
    ni = ab_im
    den = a_re * a_re + a_im * a_im
    f_re = (nr * a_re + ni * a_im) / den
    f_im = (ni * a_re - nr * a_im) / den
    abre_ref[0] = ab_re
    abim_ref[0] = ab_im
    b_re = bre_ref[0]
    b_im = bim_ref[0]
    bbre_ref[0] = f_re[None] * b_re - f_im[None] * b_im
    bbim_ref[0] = f_re[None] * b_im + f_im[None] * b_re


def _s5_discretize(a_re, a_im, log_dt, b_re, b_im):
    nd, g, n = a_re.shape
    p = b_re.shape[-1]
    bt_re = jnp.transpose(b_re, (0, 3, 1, 2))
    bt_im = jnp.transpose(b_im, (0, 3, 1, 2))
    s3 = pl.BlockSpec((1, g, n), lambda i: (i, 0, 0))
    s4 = pl.BlockSpec((1, p, g, n), lambda i: (i, 0, 0, 0))
    return pl.pallas_call(
        _s5_disc_kernel,
        grid=(nd,),
        in_specs=[s3, s3, pl.BlockSpec((1, g, 1), lambda i: (i, 0, 0)), s4, s4],
        out_specs=[s3, s3, s4, s4],
        out_shape=[jax.ShapeDtypeStruct((nd, g, n), F32)] * 2 + [jax.ShapeDtypeStruct((nd, p, g, n), F32)] * 2,
        compiler_params=_cparams("parallel"),
        name="s5_disc",
    )(a_re, a_im, log_dt.reshape(nd, g, 1), bt_re, bt_im)


def _s5_matrices(ab_re, ab_im, bb_re, bb_im, c_re, c_im):
    p, g, n = bb_re.shape
    gh = g // 2
    eye = jnp.eye(gh, dtype=F32)

    def in_half(bb, hh):
        blk = jnp.transpose(bb[:, hh * gh:(hh + 1) * gh], (1, 0, 2))
        return jnp.einsum("gpn,gk->gpkn", blk, eye).reshape(gh * p, gh * n)

    def out_half(c, hh):
        blk = c[hh * gh:(hh + 1) * gh]
        return jnp.einsum("gpn,gk->gnkp", blk, eye).reshape(gh * n, gh * p)

    bm = jnp.stack([jnp.concatenate([in_half(bb_re, hh), in_half(bb_im, hh)], axis=1) for hh in range(2)])
    cre = jnp.stack([out_half(c_re, hh) for hh in range(2)])
    cim = jnp.stack([out_half(c_im, hh) for hh in range(2)])
    return (bm.astype(BF16), ab_re.reshape(1, g * n), ab_im.reshape(1, g * n), cre.astype(BF16), cim.astype(BF16))


def _s5_scan_kernel(u_ref, bm_ref, are_ref, aim_ref, cre_ref, cim_ref, y_ref, bure, buim, hre, him, *, reverse):
    bt, t_len, w = u_ref.shape
    ns = hre.shape[1]
    half = ns // 2

    @pl.when(pl.program_id(1) == 0)
    def _():
        hre[...] = jnp.zeros_like(hre)
        him[...] = jnp.zeros_like(him)

    u_tb = jnp.swapaxes(u_ref[...].astype(F32), 0, 1).reshape(t_len * bt, w).astype(BF16)
    for hh in range(2):
        bu = _bdot(u_tb[:, hh * (w // 2):(hh + 1) * (w // 2)], bm_ref[hh])
        bure[:, hh * half:(hh + 1) * half] = bu[:, :half]
        buim[:, hh * half:(hh + 1) * half] = bu[:, half:]

    for c0 in range(0, ns, S5_COLS):
        cs = slice(c0, c0 + S5_COLS)
        a_r = jnp.broadcast_to(are_ref[:, cs], (bt, S5_COLS))
        a_i = jnp.broadcast_to(aim_ref[:, cs], (bt, S5_COLS))

        def body(t, carry, cs=cs, a_r=a_r, a_i=a_i):
            h_r, h_i = carry
            tt = (t_len - 1 - t) if reverse else t
            rows = pl.ds(pl.multiple_of(tt * bt, bt), bt)
            n_r = a_r * h_r - a_i * h_i + bure[rows, cs]
            n_i = a_r * h_i + a_i * h_r + buim[rows, cs]
            bure[rows, cs] = n_r
            buim[rows, cs] = n_i
            return n_r, n_i

        h_r, h_i = lax.fori_loop(0, t_len, body, (hre[:, cs], him[:, cs]), unroll=4)
        hre[:, cs] = h_r
        him[:, cs] = h_i

    ys = []
    for hh in range(2):
        x_r = bure[:, hh * half:(hh + 1) * half].astype(BF16)
        x_i = buim[:, hh * half:(hh + 1) * half].astype(BF16)
        ys.append(_bdot(x_r, cre_ref[hh]) - _bdot(x_i, cim_ref[hh]))
    y_tb = jnp.concatenate(ys, axis=1).reshape(t_len, bt, w)
    y_ref[...] = jnp.swapaxes(y_tb, 0, 1).astype(BF16)


def _s5_scan(u, mats, lc, reverse):
    bm, a_r, a_i, cre, cim = mats
    b, s, w = u.shape
    bt, t_len = S5_BATCH_TILE, S5_CHUNK
    nch, nctx = s // t_len, lc // t_len
    ns = a_r.shape[1]

    def chunk(k):
        if not reverse:
            return k
        return jnp.where(k < nctx, nctx - 1 - k, nch - 1 - (k - nctx))

    blk = pl.BlockSpec((bt, t_len, w), lambda g, k: (g, chunk(k), 0))
    return pl.pallas_call(
        functools.partial(_s5_scan_kernel, reverse=reverse),
        grid=(b // bt, nch),
        in_specs=[blk, _full_spec(bm.shape), _full_spec(a_r.shape), _full_spec(a_i.shape),
                  _full_spec(cre.shape), _full_spec(cim.shape)],
        out_specs=blk,
        out_shape=jax.ShapeDtypeStruct((b, s, w), BF16),
        scratch_shapes=[pltpu.VMEM((t_len * bt, ns), F32), pltpu.VMEM((t_len * bt, ns), F32),
                        pltpu.VMEM((bt, ns), F32), pltpu.VMEM((bt, ns), F32)],
        compiler_params=_cparams("parallel", "arbitrary"),
        name="s5_bwd" if reverse else "s5_fwd",
    )(u, bm, a_r, a_i, cre, cim)


def _hyb_mid_kernel(x_ref, m_ref, u_ref, yf_ref, yb_ref, zg_ref, d_ref, gw_ref, gb_ref, lg_ref, lb_ref,
                    ws_ref, bsx_ref, ow_ref, o_ref, *, nctx):
    seg = (pl.program_id(1) >= nctx).astype(jnp.int32)
    tm = x_ref.shape[1]
    w = u_ref.shape[2]
    y = u_ref[0].astype(F32) * d_ref[...] + yf_ref[0].astype(F32) + yb_ref[0].astype(F32)
    sg = _gelu_tanh(y)
    s5o = sg * _sigmoid(_bdot(sg.astype(BF16), gw_ref[...]) + gb_ref[...])
    zg = _gelu_tanh(zg_ref[0].astype(F32))
    gu = zg[:, :w]
    v = zg[:, w:]
    mu = jnp.mean(v, axis=-1, keepdims=True)
    vc = v - mu
    var = jnp.mean(vc * vc, axis=-1, keepdims=True)
    vn = (vc * lax.rsqrt(var + NORM_EPS) * lg_ref[...] + lb_ref[...]).astype(BF16)
    lane = lax.broadcasted_iota(jnp.int32, (GM_CHUNK, 128), 1)
    low = lane < (w // GM_HEADS)
    chunks = []
    for c in range(tm // GM_CHUNK):
        vch = vn[c * GM_CHUNK:(c + 1) * GM_CHUNK]
        cols = []
        for kp in range(GM_HEADS // 2):
            vs = vch[:, kp * 128:(kp + 1) * 128]
            cols.append(jnp.where(low, _bdot(ws_ref[2 * kp], vs), _bdot(ws_ref[2 * kp + 1], vs)))
        chunks.append(jnp.concatenate(cols, axis=1) + bsx_ref[...])
    gm = gu * jnp.concatenate(chunks, axis=0)
    ycat = jnp.concatenate([s5o, gm], axis=1).astype(BF16)
    o_ref[0] = x_ref[0] + m_ref[0, seg, 2:3, :] * _bdot(ycat, ow_ref[...])


def _hyb_mid(x, modsel, u, yf, yb, zg, d, glu_w, glu_b, ln_g, ln_b, ws, bsx, out_w, lc):
    b, s, dm = x.shape
    w = u.shape[2]
    tm = ROW_TILE
    vec = _full_spec((1, w))
    return pl.pallas_call(
        functools.partial(_hyb_mid_kernel, nctx=lc // tm),
        grid=(b, s // tm),
        in_specs=[_row_spec(tm, dm), _mod_spec(dm), _row_spec(tm, w), _row_spec(tm, w), _row_spec(tm, w),
                  _row_spec(tm, 2 * w), vec, _full_spec((w, w)), vec, vec, vec,
                  _full_spec(ws.shape), _full_spec(bsx.shape), _full_spec(out_w.shape)],
        out_specs=_row_spec(tm, dm),
        out_shape=jax.ShapeDtypeStruct((b, s, dm), F32),
        compiler_params=_cparams("parallel", "parallel"),
        name="hyb_mid",
    )(x, modsel, u, yf, yb, zg, d.reshape(1, w), glu_w, glu_b.reshape(1, w), ln_g.reshape(1, w),
      ln_b.reshape(1, w), ws, bsx, out_w)


def _mla_in_kernel(x_ref, m_ref, g_ref, win_ref, qng_ref, kvng_ref, wq_ref, wkv_ref, qn_ref, qr_ref, kn_ref,
                   kr_ref, rot_ref, cos_ref, sin_ref, q_out, k_out, v_out, *, nctx, q_lora, kv_lora, heads, scale):
    seg = (pl.program_id(1) >= nctx).astype(jnp.int32)
    h = _modulate(x_ref[0], g_ref[...], m_ref[0, seg, 0:1, :], m_ref[0, seg, 1:2, :])
    z = _bdot(h.astype(BF16), win_ref[...])
    cq = z[:, :q_lora]
    ckv = z[:, q_lora:q_lora + kv_lora]
    kr = z[:, q_lora + kv_lora:]
    q = _bdot((_rms(cq) * qng_ref[...]).astype(BF16), wq_ref[...])
    kv = _bdot((_rms(ckv) * kvng_ref[...]).astype(BF16), wkv_ref[...])
    cos = cos_ref[...]
    sin = sin_ref[...]

    def rope(t):
        return t * cos + _bdot(t.astype(BF16), rot_ref[...]) * sin

    k_rope = rope(_rms(kr, MLA_ROPE) * kr_ref[...]).astype(BF16)
    for hd in range(heads):
        qh = q[:, hd * MLA_QK:(hd + 1) * MLA_QK]
        q_out[0, hd, :, :MLA_NOPE] = (_rms(qh[:, :MLA_NOPE]) * qn_ref[...] * scale).astype(BF16)
        q_out[0, hd, :, MLA_NOPE:] = (rope(_rms(qh[:, MLA_NOPE:], MLA_ROPE) * qr_ref[...]) * scale).astype(BF16)
        kvh = kv[:, hd * (MLA_NOPE + MLA_V):(hd + 1) * (MLA_NOPE + MLA_V)]
        k_out[0, hd, :, :MLA_NOPE] = (_rms(kvh[:, :MLA_NOPE]) * kn_ref[...]).astype(BF16)
        k_out[0, hd, :, MLA_NOPE:] = k_rope
        v_out[0, hd] = kvh[:, MLA_NOPE:].astype(BF16)


def _mla_in(x, modsel, g, win, qng, kvng, wq, wkv, qn_g, qr_g, kn_g, kr_g, rot, cos, sin, lc, heads):
    b, s, d = x.shape
    tm = ROW_TILE
    q_lora, kv_lora = qng.shape[0], kvng.shape[0]
    scale = (MLA_NOPE + MLA_ROPE) ** -0.5 * math.log2(math.e)
    pad = jnp.zeros((MLA_QK - MLA_NOPE - MLA_ROPE,), F32)
    vec = lambda a: _full_spec((1, a.shape[-1]))
    args = [g.reshape(1, d), win, qng.reshape(1, -1), kvng.reshape(1, -1), wq, wkv, qn_g.reshape(1, -1),
            jnp.concatenate([qr_g, pad]).reshape(1, -1), kn_g.reshape(1, -1),
            jnp.concatenate([kr_g, pad]).reshape(1, -1), rot]
    tab = pl.BlockSpec((tm, 128), lambda bb, i: (i, 0))
    head_spec = lambda n: pl.BlockSpec((1, heads, tm, n), lambda bb, i: (bb, 0, i, 0))
    return pl.pallas_call(
        functools.partial(_mla_in_kernel, nctx=lc // tm, q_lora=q_lora, kv_lora=kv_lora, heads=heads, scale=scale),
        grid=(b, s // tm),
        in_specs=[_row_spec(tm, d), _mod_spec(d)] + [_full_spec(a.shape) for a in args] + [tab, tab],
        out_specs=[head_spec(MLA_QK), head_spec(MLA_QK), head_spec(MLA_V)],
        out_shape=[jax.ShapeDtypeStruct((b, heads, s, MLA_QK), BF16), jax.ShapeDtypeStruct((b, heads, s, MLA_QK), BF16),
                   jax.ShapeDtypeStruct((b, heads, s, MLA_V), BF16)],
        compiler_params=_cparams("parallel", "parallel"),
        name="mla_in",
    )(x, modsel, *args, cos, sin)


def _attn_kernel(q_ref, k_ref, v_ref, o_ref, *, lc, nq_ctx):
    i = pl.program_id(2)
    hp, s_len, dv = v_ref.shape[1], v_ref.shape[2], v_ref.shape[3]

    def run(n):
        for hd in range(hp):
            q = q_ref[0, hd]
            k = k_ref[0, hd, :n, :]
            v = v_ref[0, hd, :n, :]
            s = lax.dot_general(q, k, (((1,), (1,)), ((), ())), preferred_element_type=F32)
            p = jnp.exp2(s - jnp.max(s, axis=-1, keepdims=True))
            l = jnp.sum(p, axis=-1, keepdims=True)
            o_ref[0, :, hd * dv:(hd + 1) * dv] = (_bdot(p.astype(BF16), v) / l).astype(BF16)

    pl.when(i < nq_ctx)(lambda: run(lc))
    pl.when(i >= nq_ctx)(lambda: run(s_len))


def _attention(q, k, v, lc):
    b, heads, s, dq = q.shape
    dv = v.shape[3]
    tq = ROW_TILE
    hp = ATTN_HEADS_PER_STEP
    kv_spec = lambda n: pl.BlockSpec((1, hp, s, n), lambda bb, h, i: (bb, h, 0, 0))
    return pl.pallas_call(
        functools.partial(_attn_kernel, lc=lc, nq_ctx=lc // tq),
        grid=(b, heads // hp, s // tq),
        in_specs=[pl.BlockSpec((1, hp, tq, dq), lambda bb, h, i: (bb, h, i, 0)), kv_spec(dq), kv_spec(dv)],
        out_specs=pl.BlockSpec((1, tq, hp * dv), lambda bb, h, i: (bb, i, h)),
        out_shape=jax.ShapeDtypeStruct((b, s, heads * dv), BF16),
        compiler_params=_cparams("parallel", "parallel", "parallel"),
        name="mla_attn",
    )(q, k, v)


def _mla_out_kernel(x_ref, m_ref, o_ref, w_ref, out_ref, *, nctx):
    seg = (pl.program_id(1) >= nctx).astype(jnp.int32)
    out_ref[0] = x_ref[0] + m_ref[0, seg, 2:3, :] * _bdot(o_ref[0], w_ref[...])


def _mla_out(x, modsel, o, w, lc):
    b, s, d = x.shape
    tm = ROW_TILE
    return pl.pallas_call(
        functools.partial(_mla_out_kernel, nctx=lc // tm),
        grid=(b, s // tm),
        in_specs=[_row_spec(tm, d), _mod_spec(d), _row_spec(tm, o.shape[2]), _full_spec(w.shape)],
        out_specs=_row_spec(tm, d),
        out_shape=jax.ShapeDtypeStruct((b, s, d), F32),
        compiler_params=_cparams("parallel", "parallel"),
        name="mla_out",
    )(x, modsel, o, w)


def _router_kernel(x_ref, m_ref, g_ref, rwt_ref, rb_ref, hx_ref, meta_ref, counts_ref, run_ref, *, nctx):
    seg = (pl.program_id(1) >= nctx).astype(jnp.int32)
    tm, dm = x_ref.shape[1], x_ref.shape[2]

    @pl.when((pl.program_id(0) == 0) & (pl.program_id(1) == 0))
    def _():
        run_ref[...] = jnp.zeros_like(run_ref)

    h = _modulate(x_ref[0], g_ref[...], m_ref[0, seg, 3:4, :], m_ref[0, seg, 4:5, :])
    hx_ref[0, :, :dm] = h
    logits = lax.dot_general(rwt_ref[...], h, (((1,), (1,)), ((), ())), precision=HIGHEST,
                             preferred_element_type=F32)
    scores = _sigmoid(logits)
    sel = scores + rb_ref[...]
    sel_r = [sel[e:e + 1, :] for e in range(N_EXPERTS)]
    sc_r = [scores[e:e + 1, :] for e in range(N_EXPERTS)]

    grp = []
    for g in range(N_GROUPS):
        a, b, c, d = sel_r[GROUP_SIZE * g:GROUP_SIZE * (g + 1)]
        hi1, lo1, hi2, lo2 = jnp.maximum(a, b), jnp.minimum(a, b), jnp.maximum(c, d), jnp.minimum(c, d)
        grp.append(jnp.maximum(hi1, hi2) + jnp.maximum(jnp.minimum(hi1, hi2), jnp.maximum(lo1, lo2)))
    best = jnp.zeros((1, tm), jnp.int32)
    cur = grp[0]
    for g in range(1, N_GROUPS):
        upd = grp[g] > cur
        best = jnp.where(upd, g, best)
        cur = jnp.where(upd, grp[g], cur)

    def in_group(rows, j):
        out = rows[j]
        for g in range(1, N_GROUPS):
            out = jnp.where(best == g, rows[GROUP_SIZE * g + j], out)
        return out

    v = [in_group(sel_r, j) for j in range(GROUP_SIZE)]
    sc = [in_group(sc_r, j) for j in range(GROUP_SIZE)]

    def first_argmax(vals):
        idx = jnp.zeros((1, tm), jnp.int32)
        top = vals[0]
        for j in range(1, GROUP_SIZE):
            upd = vals[j] > top
            idx = jnp.where(upd, j, idx)
            top = jnp.where(upd, vals[j], top)
        return idx

    i1 = first_argmax(v)
    i2 = first_argmax([jnp.where(i1 == j, -jnp.inf, v[j]) for j in range(GROUP_SIZE)])
    w1 = sum(jnp.where(i1 == j, sc[j], 0.0) for j in range(GROUP_SIZE))
    w2 = sum(jnp.where(i2 == j, sc[j], 0.0) for j in range(GROUP_SIZE))
    den = w1 + w2
    first_lo = i1 < i2
    lo = jnp.minimum(i1, i2)
    hi = jnp.maximum(i1, i2)
    w_lo = jnp.where(first_lo, w1, w2) / den
    w_hi = jnp.where(first_lo, w2, w1) / den
    pair = jnp.where(lo == 0, hi - 1, jnp.where(lo == 1, hi + 1, 5))
    cls = best * 6 + pair

    crow = lax.broadcasted_iota(jnp.int32, (CLASS_ROWS, tm), 0)
    onehot = crow == cls
    tri = (lax.broadcasted_iota(jnp.int32, (tm, tm), 0) <= lax.broadcasted_iota(jnp.int32, (tm, tm), 1))
    prefix = _bdot(jnp.where(onehot, 1.0, 0.0).astype(BF16), jnp.where(tri, 1.0, 0.0).astype(BF16))
    run = run_ref[...]
    rank = jnp.sum(jnp.where(onehot, prefix - 1.0 + run[:, 0:1], 0.0), axis=0, keepdims=True)
    run = run + prefix[:, tm - 1:tm]
    run_ref[...] = run
    counts_ref[...] = run

    r8 = lax.broadcasted_iota(jnp.int32, (8, tm), 0)
    meta_ref[...] = jnp.where(r8 == 0, cls.astype(F32), jnp.where(r8 == 1, rank, 0.0))
    rl = lax.broadcasted_iota(jnp.int32, (LANES, tm), 0)
    wt = jnp.where(rl == 0, w_lo, jnp.where(rl == 1, w_hi, 0.0))
    hx_ref[0, :, dm:] = wt.T


def _router(x, modsel, g, router_w, router_bias, lc):
    b, s, d = x.shape
    tm = ROW_TILE
    nt = s // tm
    return pl.pallas_call(
        functools.partial(_router_kernel, nctx=lc // tm),
        grid=(b, nt),
        in_specs=[_row_spec(tm, d), _mod_spec(d), _full_spec((1, d)), _full_spec((N_EXPERTS, d)),
                  _full_spec((N_EXPERTS, 1))],
        out_specs=[_row_spec(tm, d + LANES), pl.BlockSpec((8, tm), lambda bb, i: (0, bb * nt + i)),
                   _full_spec((CLASS_ROWS, LANES))],
        out_shape=[jax.ShapeDtypeStruct((b, s, d + LANES), F32), jax.ShapeDtypeStruct((8, b * s), F32),
                   jax.ShapeDtypeStruct((CLASS_ROWS, LANES), F32)],
        scratch_shapes=[pltpu.VMEM((CLASS_ROWS, LANES), F32)],
        compiler_params=_cparams("arbitrary", "arbitrary"),
        name="moe_router",
    )(x, modsel, g.reshape(1, d), router_w.T, router_bias.reshape(N_EXPERTS, 1))


def _row_copy(src, dst, sem):
    return pltpu.make_async_copy(src, dst, sem)


def _zero_fill(zpos_ref, zlen_ref, zbuf, xs_ref, sem, wait):
    top = zbuf.shape[0] // SUBLANES
    for c in range(zpos_ref.shape[0]):
        units = zlen_ref[c]
        done = 0
        bit = top
        while bit >= 1:
            rows = bit * SUBLANES
            take = units & bit

            @pl.when(take != 0)
            def _(rows=rows, done=done):
                start = pl.multiple_of(zpos_ref[c] + done * SUBLANES, SUBLANES)
                cp = _row_copy(zbuf.at[pl.ds(0, rows)], xs_ref.at[pl.ds(start, rows)], sem)
                cp.wait() if wait else cp.start()

            done = done + take
            bit //= 2


def _dispatch_kernel(zpos_ref, zlen_ref, pos_ref, hx_ref, xs_ref, zbuf, sem):
    tm = hx_ref.shape[0]

    @pl.when(pl.program_id(0) == 0)
    def _():
        zbuf[...] = jnp.zeros_like(zbuf)
        _zero_fill(zpos_ref, zlen_ref, zbuf, xs_ref, sem, False)
        _zero_fill(zpos_ref, zlen_ref, zbuf, xs_ref, sem, True)

    def start(r, carry):
        _row_copy(hx_ref.at[pl.ds(r, 1)], xs_ref.at[pl.ds(pos_ref[0, 0, r], 1)], sem).start()
        return carry

    lax.fori_loop(0, tm, start, 0, unroll=8)
    _row_copy(hx_ref, xs_ref.at[pl.ds(0, tm)], sem).wait()


def _dispatch(hx, pos, zpos, zlen, n_rows):
    n, dx = hx.shape
    tm = DISPATCH_TILE
    return pl.pallas_call(
        _dispatch_kernel,
        grid_spec=pltpu.PrefetchScalarGridSpec(
            num_scalar_prefetch=2,
            grid=(n // tm,),
            in_specs=[pl.BlockSpec((1, 1, tm), lambda i, zp, zl: (i, 0, 0), memory_space=pltpu.SMEM),
                      pl.BlockSpec((tm, dx), lambda i, zp, zl: (i, 0))],
            out_specs=pl.BlockSpec(memory_space=pl.ANY),
            scratch_shapes=[pltpu.VMEM((EXP_TILE, dx), F32), pltpu.SemaphoreType.DMA(())],
        ),
        out_shape=jax.ShapeDtypeStruct((n_rows, dx), F32),
        compiler_params=_cparams("arbitrary"),
        name="moe_dispatch",
    )(zpos, zlen, pos.reshape(n // tm, 1, tm), hx)


def _experts_kernel(tblk_ref, elo_ref, ehi_ref, valid_ref, xs_ref, w1l, w3l, w2l, w1h, w3h, w2h, ys_ref):
    j = pl.program_id(0)
    d = ys_ref.shape[1]

    @pl.when(valid_ref[j] == 1)
    def _():
        x = xs_ref[:, :d].astype(BF16)

        def hidden(w1, w3, w):
            a = _bdot(x, w1[0])
            return ((a * _sigmoid(a)) * _bdot(x, w3[0]) * w).astype(BF16)

        ys_ref[...] = (_bdot(hidden(w1l, w3l, xs_ref[:, d:d + 1]), w2l[0])
                       + _bdot(hidden(w1h, w3h, xs_ref[:, d + 1:d + 2]), w2h[0]))

    @pl.when(valid_ref[j] == 0)
    def _():
        ys_ref[...] = jnp.zeros_like(ys_ref)


def _experts(xs, tblk, elo, ehi, valid, w1, w3, w2, n_tiles):
    dx = xs.shape[1]
    ne, d, f = w1.shape
    tm = EXP_TILE
    wlo = lambda shape: pl.BlockSpec(shape, lambda j, tb, lo, hi, va: (lo[j], 0, 0))
    whi = lambda shape: pl.BlockSpec(shape, lambda j, tb, lo, hi, va: (hi[j], 0, 0))
    return pl.pallas_call(
        _experts_kernel,
        grid_spec=pltpu.PrefetchScalarGridSpec(
            num_scalar_prefetch=4,
            grid=(n_tiles,),
            in_specs=[pl.BlockSpec((tm, dx), lambda j, tb, lo, hi, va: (tb[j], 0)),
                      wlo((1, d, f)), wlo((1, d, f)), wlo((1, f, d)), whi((1, d, f)), whi((1, d, f)), whi((1, f, d))],
            out_specs=pl.BlockSpec((tm, d), lambda j, tb, lo, hi, va: (j, 0)),
        ),
        out_shape=jax.ShapeDtypeStruct((n_tiles * tm, d), F32),
        compiler_params=_cparams("arbitrary"),
        name="moe_experts",
    )(tblk, elo, ehi, valid, xs, w1, w3, w2, w1, w3, w2)


def _combine_kernel(pos_ref, posn_ref, x_ref, m_ref, ys_ref, o_ref, buf, sem, *, nt, nctx):
    t = pl.program_id(0)
    n = pl.num_programs(0)
    tm = x_ref.shape[1]
    slot = t % 2

    def gather(p_ref, sl):
        def body(r, carry):
            _row_copy(ys_ref.at[pl.ds(p_ref[0, 0, r], 1)], buf.at[sl, pl.ds(r, 1)], sem.at[sl]).start()
            return carry
        lax.fori_loop(0, tm, body, 0, unroll=8)

    @pl.when(t == 0)
    def _():
        gather(pos_ref, 0)

    @pl.when(t + 1 < n)
    def _():
        gather(posn_ref, 1 - slot)

    _row_copy(ys_ref.at[pl.ds(0, tm)], buf.at[slot], sem.at[slot]).wait()
    seg = ((t % nt) >= nctx).astype(jnp.int32)
    o_ref[0] = x_ref[0] + m_ref[0, seg, 5:6, :] * buf[slot]


def _combine(x, modsel, ys, pos, lc):
    b, s, d = x.shape
    tm = ROW_TILE
    nt = s // tm
    n = b * nt
    pos3 = pos.reshape(n, 1, tm)
    return pl.pallas_call(
        functools.partial(_combine_kernel, nt=nt, nctx=lc // tm),
        grid=(n,),
        in_specs=[pl.BlockSpec((1, 1, tm), lambda t: (t, 0, 0), memory_space=pltpu.SMEM),
                  pl.BlockSpec((1, 1, tm), lambda t: (jnp.minimum(t + 1, n - 1), 0, 0), memory_space=pltpu.SMEM),
                  pl.BlockSpec((1, tm, d), lambda t: (t // nt, t % nt, 0)),
                  pl.BlockSpec((1, 2, 6, d), lambda t: (t // nt, 0, 0, 0)),
                  pl.BlockSpec(memory_space=pl.ANY)],
        out_specs=pl.BlockSpec((1, tm, d), lambda t: (t // nt, t % nt, 0)),
        out_shape=jax.ShapeDtypeStruct((b, s, d), F32),
        scratch_shapes=[pltpu.VMEM((2, tm, d), F32), pltpu.SemaphoreType.DMA((2,))],
        compiler_params=_cparams("arbitrary"),
        name="moe_combine",
    )(pos3, pos3, x, modsel, ys)


def _class_experts():
    lo, hi = [], []
    for g in range(N_GROUPS):
        for a in range(GROUP_SIZE):
            for c in range(a + 1, GROUP_SIZE):
                lo.append(GROUP_SIZE * g + a)
                hi.append(GROUP_SIZE * g + c)
    return jnp.array(lo, jnp.int32), jnp.array(hi, jnp.int32)


def _moe(x, modsel, g, router_w, router_bias, w1, w3, w2, lc):
    b, s, d = x.shape
    n = b * s
    tm = EXP_TILE
    hx, meta, counts = _router(x, modsel, g, router_w, router_bias, lc)

    cnt = counts[:N_CLASSES, 0].astype(jnp.int32)
    padded = (cnt + tm - 1) // tm * tm
    ends = jnp.cumsum(padded)
    off = ends - padded
    pos = off[meta[0].astype(jnp.int32)] + meta[1].astype(jnp.int32)
    n_tiles = n // tm + N_CLASSES
    start = jnp.arange(n_tiles, dtype=jnp.int32) * tm
    valid = start < ends[-1]
    last = ends[-1] // tm - 1
    tblk = jnp.minimum(jnp.arange(n_tiles, dtype=jnp.int32), last)
    tcls = jnp.minimum(jnp.searchsorted(ends, tblk * tm, side="right"), N_CLASSES - 1).astype(jnp.int32)
    lo_tab, hi_tab = _class_experts()

    n_rows = n_tiles * tm
    pad0 = (off + cnt) // SUBLANES * SUBLANES
    tail = ends[-1] + jnp.arange(N_CLASSES, dtype=jnp.int32) * tm
    zpos = jnp.concatenate([pad0, jnp.minimum(tail, n_rows - tm)]).astype(jnp.int32)
    zlen = jnp.concatenate([(ends - pad0) // SUBLANES, jnp.where(tail < n_rows, tm // SUBLANES, 0)]).astype(jnp.int32)
    xs = _dispatch(hx.reshape(n, d + LANES), pos, zpos, zlen, n_rows)
    ys = _experts(xs, tblk, lo_tab[tcls], hi_tab[tcls], valid.astype(jnp.int32), w1, w3, w2, n_tiles)
    return _combine(x, modsel, ys, pos, lc)


def _rope_tables(lc, l):
    half = MLA_ROPE // 4
    freqs = ROPE_BASE ** (-jnp.arange(half, dtype=F32) / half)
    pos = jnp.arange(l)
    pos_row = (pos // GRID_W).astype(F32)
    pos_col = (pos % GRID_W).astype(F32)
    ang = jnp.concatenate([pos_row[:, None] * freqs, pos_row[:, None] * freqs,
                           pos_col[:, None] * freqs, pos_col[:, None] * freqs], axis=1)
    pad = jnp.zeros((l, 128 - MLA_ROPE), F32)
    cos = jnp.concatenate([jnp.concatenate([jnp.ones((lc, MLA_ROPE), F32), jnp.zeros((lc, 128 - MLA_ROPE), F32)], axis=1),
                           jnp.concatenate([jnp.cos(ang), pad], axis=1)], axis=0)
    sin = jnp.concatenate([jnp.zeros((lc, 128), F32), jnp.concatenate([jnp.sin(ang), pad], axis=1)], axis=0)
    r = jnp.zeros((128, 128), F32)
    idx = jnp.arange(half)
    for base in (0, MLA_ROPE // 2):
        r = r.at[base + half + idx, base + idx].set(-1.0)
        r = r.at[base + idx, base + half + idx].set(1.0)
    return cos, sin, r.astype(BF16)


def _pad_heads(w, heads):
    k = w.shape[0]
    w = w.reshape(k, heads, MLA_NOPE + MLA_ROPE)
    w = jnp.concatenate([w, jnp.zeros((k, heads, MLA_QK - MLA_NOPE - MLA_ROPE), w.dtype)], axis=2)
    return w.reshape(k, heads * MLA_QK)


def kernel(x, c, ctx, c_ctx, mod_w, mod_b, norm1_g, norm2_g, hyb_in_w, hyb_out_w, s5_a_re, s5_a_im, s5_log_dt, s5_b_re, s5_b_im, s5_c_re, s5_c_im, s5_d, s5_glu_w, s5_glu_b, gm_ln_g, gm_ln_b, gm_ws, gm_bs, mla_in_w, mla_q_norm_g, mla_kv_norm_g, mla_wq_b, mla_wkv_b, mla_qn_g, mla_qr_g, mla_kn_g, mla_kr_g, mla_out_w, router_w, router_bias, moe_w1, moe_w3, moe_w2):
    b, l, d = x.shape
    lc = ctx.shape[1]
    depth = mod_w.shape[0]
    heads = mla_out_w.shape[1] // MLA_V
    assert lc % ROW_TILE == 0 and l % ROW_TILE == 0 and b % S5_BATCH_TILE == 0

    xs = jnp.concatenate([ctx, x], axis=1)

    bp = -(-(b + 1) // 8) * 8
    cond = jnp.concatenate([c, c_ctx[None], jnp.zeros((bp - b - 1, d), F32)], axis=0)
    mods = _adaln(cond, mod_w, mod_b)

    cos, sin, rot = _rope_tables(lc, l)

    for layer in range(depth):
        i = layer // 2
        m = mods[layer].reshape(bp, 6, d)
        modsel = jnp.stack([jnp.broadcast_to(m[b][None], (b, 6, d)), m[:b]], axis=1)

        if layer % 2 == 0:
            u, zg = _hyb_in(xs, modsel, norm1_g[layer], hyb_in_w[i].astype(BF16), lc)
            ab_re, ab_im, bb_re, bb_im = _s5_discretize(s5_a_re[i], s5_a_im[i], s5_log_dt[i], s5_b_re[i], s5_b_im[i])
            ys = []
            for dr in range(2):
                mats = _s5_matrices(ab_re[dr], ab_im[dr], bb_re[dr], bb_im[dr], s5_c_re[i, dr], s5_c_im[i, dr])
                ys.append(_s5_scan(u, mats, lc, reverse=bool(dr)))
            hd = (d // 2) // GM_HEADS
            bsx = jnp.repeat(gm_bs[i].T, hd, axis=1)
            xs = _hyb_mid(xs, modsel, u, ys[0], ys[1], zg, s5_d[i], s5_glu_w[i].astype(BF16), s5_glu_b[i],
                          gm_ln_g[i], gm_ln_b[i], gm_ws[i].astype(BF16), bsx, hyb_out_w[i].astype(BF16), lc)
        else:
            q_lora = mla_q_norm_g.shape[1]
            kv_lora = mla_kv_norm_g.shape[1]
            win = jnp.concatenate([mla_in_w[i], jnp.zeros((d, 128 - MLA_ROPE), F32)], axis=1).astype(BF16)
            wq = _pad_heads(mla_wq_b[i], heads).astype(BF16)
            q, k, v = _mla_in(xs, modsel, norm1_g[layer], win, mla_q_norm_g[i], mla_kv_norm_g[i], wq,
                              mla_wkv_b[i].astype(BF16), mla_qn_g[i], mla_qr_g[i], mla_kn_g[i], mla_kr_g[i],
                              rot, cos, sin, lc, heads)
            o = _attention(q, k, v, lc)
            xs = _mla_out(xs, modsel, o, mla_out_w[i].astype(BF16), lc)

        xs = _moe(xs, modsel, norm2_g[layer], router_w, router_bias, moe_w1[layer].astype(BF16),
                  moe_w3[layer].astype(BF16), moe_w2[layer].astype(BF16), lc)

    return xs[:, lc:, :]
```

```python
import functools
import math

import jax
import jax.numpy as jnp
from jax import lax
from jax.experimental import pallas as pl
from jax.experimental.pallas import tpu as pltpu

F32 = jnp.float32
BF16 = jnp.bfloat16
HIGHEST = lax.Precision.HIGHEST

NORM_EPS = 1e-6
ROPE_BASE = 10000.0
GRID_W = 64

S5_GROUP_CH = 16
S5_STATE = 64
S5_BATCH_TILE = 8
S5_CHUNK = 128
S5_COLS = 512

GM_HEADS = 8
GM_CHUNK = 128

MLA_NOPE = 128
MLA_ROPE = 64
MLA_V = 128
MLA_QK = 256

N_EXPERTS = 16
N_GROUPS = 4
GROUP_SIZE = 4

ROW_TILE = 256
ATTN_HEADS_PER_STEP = 4
N_CLASSES = 24
CLASS_ROWS = 32
EXP_TILE = 512
DISPATCH_TILE = 512
LANES = 128
SUBLANES = 8
VMEM_LIMIT = 56 * 1024 * 1024


def _cparams(*sem):
    return pltpu.CompilerParams(dimension_semantics=sem, vmem_limit_bytes=VMEM_LIMIT)


def _sigmoid(x):
    return 1.0 / (1.0 + jnp.exp(-x))


def _gelu_tanh(x):
    c = math.sqrt(2.0 / math.pi)
    return 0.5 * x * (1.0 + jnp.tanh(c * (x + 0.044715 * (x * x * x))))


def _rms(x, n=None):
    n = x.shape[-1] if n is None else n
    return x * lax.rsqrt(jnp.sum(x * x, axis=-1, keepdims=True) * (1.0 / n) + NORM_EPS)


def _modulate(x, g, shift, scale):
    return (_rms(x) * g) * (1.0 + scale) + shift


def _bdot(a, b):
    return jnp.dot(a, b, preferred_element_type=F32)


def _adaln_kernel(c_ref, w_ref, b_ref, o_ref):
    c = c_ref[...]
    s = c * _sigmoid(c)
    o_ref[0] = jnp.dot(s, w_ref[0], precision=HIGHEST, preferred_element_type=F32) + b_ref[0]


def _adaln(cond, mod_w, mod_b):
    depth, d, n = mod_w.shape
    bp = cond.shape[0]
    tn = n // 4
    return pl.pallas_call(
        _adaln_kernel,
        grid=(depth, n // tn),
        in_specs=[
            pl.BlockSpec((bp, d), lambda l, j: (0, 0)),
            pl.BlockSpec((1, d, tn), lambda l, j: (l, 0, j)),
            pl.BlockSpec((1, 1, tn), lambda l, j: (l, 0, j)),
        ],
        out_specs=pl.BlockSpec((1, bp, tn), lambda l, j: (l, 0, j)),
        out_shape=jax.ShapeDtypeStruct((depth, bp, n), F32),
        compiler_params=_cparams("parallel", "parallel"),
        name="adaln",
    )(cond, mod_w, mod_b.reshape(depth, 1, n))


def _mod_spec(d):
    return pl.BlockSpec((1, 2, 6, d), lambda b, i: (b, 0, 0, 0))


def _row_spec(tm, n):
    return pl.BlockSpec((1, tm, n), lambda b, i: (b, i, 0))


def _full_spec(shape):
    nd = len(shape)
    return pl.BlockSpec(shape, lambda b, i: (0,) * nd)


def _hyb_in_kernel(x_ref, m_ref, g_ref, w_ref, u_ref, zg_ref, *, nctx, s5w):
    seg = (pl.program_id(1) >= nctx).astype(jnp.int32)
    h = _modulate(x_ref[0], g_ref[...], m_ref[0, seg, 0:1, :], m_ref[0, seg, 1:2, :])
    z = _bdot(h.astype(BF16), w_ref[...])
    u_ref[0] = z[:, :s5w].astype(BF16)
    zg_ref[0] = z[:, s5w:].astype(BF16)


def _hyb_in(x, modsel, g, w, lc):
    b, s, d = x.shape
    n = w.shape[1]
    s5w = d // 2
    tm = ROW_TILE
    return pl.pallas_call(
        functools.partial(_hyb_in_kernel, nctx=lc // tm, s5w=s5w),
        grid=(b, s // tm),
        in_specs=[_row_spec(tm, d), _mod_spec(d), _full_spec((1, d)), _full_spec((d, n))],
        out_specs=[_row_spec(tm, s5w), _row_spec(tm, n - s5w)],
        out_shape=[jax.ShapeDtypeStruct((b, s, s5w), BF16), jax.ShapeDtypeStruct((b, s, n - s5w), BF16)],
        compiler_params=_cparams("parallel", "parallel"),
        name="hyb_in",
    )(x, modsel, g.reshape(1, d), w)


def _s5_disc_kernel(are_ref, aim_ref, ldt_ref, bre_ref, bim_ref, abre_ref, abim_ref, bbre_ref, bbim_ref):
    a_re = are_ref[0]
    a_im = aim_ref[0]
    dt = jnp.exp(ldt_ref[0])
    mag = jnp.exp(dt * a_re)
    ab_re = mag * jnp.cos(dt * a_im)
    ab_im = mag * jnp.sin(dt * a_im)
    nr = ab_re - 1.0
    ni = ab_im
    den = a_re * a_re + a_im * a_im
    f_re = (nr * a_re + ni * a_im) / den
    f_im = (ni * a_re - nr * a_im) / den
    abre_ref[0] = ab_re
    abim_ref[0] = ab_im
    b_re = bre_ref[0]
    b_im = bim_ref[0]
    bbre_ref[0] = f_re[None] * b_re - f_im[None] * b_im
    bbim_ref[0] = f_re[None] * b_im + f_im[None] * b_re


def _s5_discretize(a_re, a_im, log_dt, b_re, b_im):
    nd, g, n = a_re.shape
    p = b_re.shape[-1]
    bt_re = jnp.transpose(b_re, (0, 3, 1, 2))
    bt_im = jnp.transpose(b_im, (0, 3, 1, 2))
    s3 = pl.BlockSpec((1, g, n), lambda i: (i, 0, 0))
    s4 = pl.BlockSpec((1, p, g, n), lambda i: (i, 0, 0, 0))
    return pl.pallas_call(
        _s5_disc_kernel,
        grid=(nd,),
        in_specs=[s3, s3, pl.BlockSpec((1, g, 1), lambda i: (i, 0, 0)), s4, s4],
        out_specs=[s3, s3, s4, s4],
        out_shape=[jax.ShapeDtypeStruct((nd, g, n), F32)] * 2 + [jax.ShapeDtypeStruct((nd, p, g, n), F32)] * 2,
        compiler_params=_cparams("parallel"),
        name="s5_disc",
    )(a_re, a_im, log_dt.reshape(nd, g, 1), bt_re, bt_im)


def _s5_matrices(ab_re, ab_im, bb_re, bb_im, c_re, c_im):
    p, g, n = bb_re.shape
    gh = g // 2
    eye = jnp.eye(gh, dtype=F32)

    def in_half(bb, hh):
        blk = jnp.transpose(bb[:, hh * gh:(hh + 1) * gh], (1, 0, 2))
        return jnp.einsum("gpn,gk->gpkn", blk, eye).reshape(gh * p, gh * n)

    def out_half(c, hh):
        blk = c[hh * gh:(hh + 1) * gh]
        return jnp.einsum("gpn,gk->gnkp", blk, eye).reshape(gh * n, gh * p)

    bm = jnp.stack([jnp.concatenate([in_half(bb_re, hh), in_half(bb_im, hh)], axis=1) for hh in range(2)])
    cre = jnp.stack([out_half(c_re, hh) for hh in range(2)])
    cim = jnp.stack([out_half(c_im, hh) for hh in range(2)])
    return (bm.astype(BF16), ab_re.reshape(1, g * n), ab_im.reshape(1, g * n), cre.astype(BF16), cim.astype(BF16))


def _s5_scan_kernel(u_ref, bm_ref, are_ref, aim_ref, cre_ref, cim_ref, y_ref, bure, buim, hre, him, *, reverse):
    bt, t_len, w = u_ref.shape
    ns = hre.shape[1]
    half = ns // 2

    @pl.when(pl.program_id(1) == 0)
    def _():
        hre[...] = jnp.zeros_like(hre)
        him[...] = jnp.zeros_like(him)

    u_tb = jnp.swapaxes(u_ref[...].astype(F32), 0, 1).reshape(t_len * bt, w).astype(BF16)
    ys = []
    for hh in range(2):
        hs = slice(hh * half, (hh + 1) * half)
        bu = _bdot(u_tb[:, hh * (w // 2):(hh + 1) * (w // 2)], bm_ref[hh])
        bure[:, hs] = bu[:, :half]
        buim[:, hs] = bu[:, half:]
        for c0 in range(hh * half, (hh + 1) * half, S5_COLS):
            cs = slice(c0, c0 + S5_COLS)
            a_r = jnp.broadcast_to(are_ref[:, cs], (bt, S5_COLS))
            a_i = jnp.broadcast_to(aim_ref[:, cs], (bt, S5_COLS))
            h_r, h_i = hre[:, cs], him[:, cs]
            for t in range(t_len):
                tt = (t_len - 1 - t) if reverse else t
                rows = slice(tt * bt, (tt + 1) * bt)
                h_r, h_i = (a_r * h_r - a_i * h_i + bure[rows, cs], a_r * h_i + a_i * h_r + buim[rows, cs])
                bure[rows, cs] = h_r
                buim[rows, cs] = h_i
            hre[:, cs] = h_r
            him[:, cs] = h_i
        ys.append(_bdot(bure[:, hs].astype(BF16), cre_ref[hh]) - _bdot(buim[:, hs].astype(BF16), cim_ref[hh]))
    y_tb = jnp.concatenate(ys, axis=1).reshape(t_len, bt, w)
    y_ref[...] = jnp.swapaxes(y_tb, 0, 1).astype(BF16)


def _s5_scan(u, mats, lc, reverse):
    bm, a_r, a_i, cre, cim = mats
    b, s, w = u.shape
    bt, t_len = S5_BATCH_TILE, S5_CHUNK
    nch, nctx = s // t_len, lc // t_len
    ns = a_r.shape[1]

    def chunk(k):
        if not reverse:
            return k
        return jnp.where(k < nctx, nctx - 1 - k, nch - 1 - (k - nctx))

    blk = pl.BlockSpec((bt, t_len, w), lambda g, k: (g, chunk(k), 0))
    return pl.pallas_call(
        functools.partial(_s5_scan_kernel, reverse=reverse),
        grid=(b // bt, nch),
        in_specs=[blk, _full_spec(bm.shape), _full_spec(a_r.shape), _full_spec(a_i.shape),
                  _full_spec(cre.shape), _full_spec(cim.shape)],
        out_specs=blk,
        out_shape=jax.ShapeDtypeStruct((b, s, w), BF16),
        scratch_shapes=[pltpu.VMEM((t_len * bt, ns), F32), pltpu.VMEM((t_len * bt, ns), F32),
                        pltpu.VMEM((bt, ns), F32), pltpu.VMEM((bt, ns), F32)],
        compiler_params=_cparams("parallel", "arbitrary"),
        name="s5_bwd" if reverse else "s5_fwd",
    )(u, bm, a_r, a_i, cre, cim)


def _hyb_mid_kernel(x_ref, m_ref, u_ref, yf_ref, yb_ref, zg_ref, d_ref, gw_ref, gb_ref, lg_ref, lb_ref,
                    ws_ref, bsx_ref, ow_ref, o_ref, *, nctx):
    seg = (pl.program_id(1) >= nctx).astype(jnp.int32)
    tm = x_ref.shape[1]
    w = u_ref.shape[2]
    y = u_ref[0].astype(F32) * d_ref[...] + yf_ref[0].astype(F32) + yb_ref[0].astype(F32)
    sg = _gelu_tanh(y)
    s5o = sg * _sigmoid(_bdot(sg.astype(BF16), gw_ref[...]) + gb_ref[...])
    zg = _gelu_tanh(zg_ref[0].astype(F32))
    gu = zg[:, :w]
    v = zg[:, w:]
    mu = jnp.mean(v, axis=-1, keepdims=True)
    vc = v - mu
    var = jnp.mean(vc * vc, axis=-1, keepdims=True)
    vn = (vc * lax.rsqrt(var + NORM_EPS) * lg_ref[...] + lb_ref[...]).astype(BF16)
    lane = lax.broadcasted_iota(jnp.int32, (GM_CHUNK, 128), 1)
    low = lane < (w // GM_HEADS)
    chunks = []
    for c in range(tm // GM_CHUNK):
        vch = vn[c * GM_CHUNK:(c + 1) * GM_CHUNK]
        cols = []
        for kp in range(GM_HEADS // 2):
            vs = vch[:, kp * 128:(kp + 1) * 128]
            cols.append(jnp.where(low, _bdot(ws_ref[2 * kp], vs), _bdot(ws_ref[2 * kp + 1], vs)))
        chunks.append(jnp.concatenate(cols, axis=1) + bsx_ref[...])
    gm = gu * jnp.concatenate(chunks, axis=0)
    ycat = jnp.concatenate([s5o, gm], axis=1).astype(BF16)
    o_ref[0] = x_ref[0] + m_ref[0, seg, 2:3, :] * _bdot(ycat, ow_ref[...])


def _hyb_mid(x, modsel, u, yf, yb, zg, d, glu_w, glu_b, ln_g, ln_b, ws, bsx, out_w, lc):
    b, s, dm = x.shape
    w = u.shape[2]
    tm = ROW_TILE
    vec = _full_spec((1, w))
    return pl.pallas_call(
        functools.partial(_hyb_mid_kernel, nctx=lc // tm),
        grid=(b, s // tm),
        in_specs=[_row_spec(tm, dm), _mod_spec(dm), _row_spec(tm, w), _row_spec(tm, w), _row_spec(tm, w),
                  _row_spec(tm, 2 * w), vec, _full_spec((w, w)), vec, vec, vec,
                  _full_spec(ws.shape), _full_spec(bsx.shape), _full_spec(out_w.shape)],
        out_specs=_row_spec(tm, dm),
        out_shape=jax.ShapeDtypeStruct((b, s, dm), F32),
        compiler_params=_cparams("parallel", "parallel"),
        name="hyb_mid",
    )(x, modsel, u, yf, yb, zg, d.reshape(1, w), glu_w, glu_b.reshape(1, w), ln_g.reshape(1, w),
      ln_b.reshape(1, w), ws, bsx, out_w)


def _mla_in_kernel(x_ref, m_ref, g_ref, win_ref, qng_ref, kvng_ref, wq_ref, wkv_ref, qn_ref, qr_ref, kn_ref,
                   kr_ref, rot_ref, cos_ref, sin_ref, q_out, k_out, v_out, *, nctx, q_lora, kv_lora, heads, scale):
    seg = (pl.program_id(1) >= nctx).astype(jnp.int32)
    h = _modulate(x_ref[0], g_ref[...], m_ref[0, seg, 0:1, :], m_ref[0, seg, 1:2, :])
    z = _bdot(h.astype(BF16), win_ref[...])
    cq = z[:, :q_lora]
    ckv = z[:, q_lora:q_lora + kv_lora]
    kr = z[:, q_lora + kv_lora:]
    q = _bdot((_rms(cq) * qng_ref[...]).astype(BF16), wq_ref[...])
    kv = _bdot((_rms(ckv) * kvng_ref[...]).astype(BF16), wkv_ref[...])
    cos = cos_ref[...]
    sin = sin_ref[...]

    def rope(t):
        return t * cos + _bdot(t.astype(BF16), rot_ref[...]) * sin

    k_rope = rope(_rms(kr, MLA_ROPE) * kr_ref[...]).astype(BF16)
    for hd in range(heads):
        qh = q[:, hd * MLA_QK:(hd + 1) * MLA_QK]
        q_out[0, hd, :, :MLA_NOPE] = (_rms(qh[:, :MLA_NOPE]) * qn_ref[...] * scale).astype(BF16)
        q_out[0, hd, :, MLA_NOPE:] = (rope(_rms(qh[:, MLA_NOPE:], MLA_ROPE) * qr_ref[...]) * scale).astype(BF16)
        kvh = kv[:, hd * (MLA_NOPE + MLA_V):(hd + 1) * (MLA_NOPE + MLA_V)]
        k_out[0, hd, :, :MLA_NOPE] = (_rms(kvh[:, :MLA_NOPE]) * kn_ref[...]).astype(BF16)
        k_out[0, hd, :, MLA_NOPE:] = k_rope
        v_out[0, hd] = kvh[:, MLA_NOPE:].astype(BF16)


def _mla_in(x, modsel, g, win, qng, kvng, wq, wkv, qn_g, qr_g, kn_g, kr_g, rot, cos, sin, lc, heads):
    b, s, d = x.shape
    tm = ROW_TILE
    q_lora, kv_lora = qng.shape[0], kvng.shape[0]
    scale = (MLA_NOPE + MLA_ROPE) ** -0.5 * math.log2(math.e)
    pad = jnp.zeros((MLA_QK - MLA_NOPE - MLA_ROPE,), F32)
    vec = lambda a: _full_spec((1, a.shape[-1]))
    args = [g.reshape(1, d), win, qng.reshape(1, -1), kvng.reshape(1, -1), wq, wkv, qn_g.reshape(1, -1),
            jnp.concatenate([qr_g, pad]).reshape(1, -1), kn_g.reshape(1, -1),
            jnp.concatenate([kr_g, pad]).reshape(1, -1), rot]
    tab = pl.BlockSpec((tm, 128), lambda bb, i: (i, 0))
    head_spec = lambda n: pl.BlockSpec((1, heads, tm, n), lambda bb, i: (bb, 0, i, 0))
    return pl.pallas_call(
        functools.partial(_mla_in_kernel, nctx=lc // tm, q_lora=q_lora, kv_lora=kv_lora, heads=heads, scale=scale),
        grid=(b, s // tm),
        in_specs=[_row_spec(tm, d), _mod_spec(d)] + [_full_spec(a.shape) for a in args] + [tab, tab],
        out_specs=[head_spec(MLA_QK), head_spec(MLA_QK), head_spec(MLA_V)],
        out_shape=[jax.ShapeDtypeStruct((b, heads, s, MLA_QK), BF16), jax.ShapeDtypeStruct((b, heads, s, MLA_QK), BF16),
                   jax.ShapeDtypeStruct((b, heads, s, MLA_V), BF16)],
        compiler_params=_cparams("parallel", "parallel"),
        name="mla_in",
    )(x, modsel, *args, cos, sin)


def _attn_kernel(q_ref, k_ref, v_ref, o_ref, *, lc, nq_ctx):
    i = pl.program_id(2)
    hp, s_len, dv = v_ref.shape[1], v_ref.shape[2], v_ref.shape[3]

    def run(n):
        for hd in range(hp):
            q = q_ref[0, hd]
            s = lax.dot_general(q, k_ref[0, hd, :n, :], (((1,), (1,)), ((), ())), preferred_element_type=F32)
            p = jnp.exp2(s - jnp.max(s, axis=-1, keepdims=True))
            l = jnp.sum(p, axis=-1, keepdims=True)
            o_ref[0, :, hd * dv:(hd + 1) * dv] = (_bdot(p.astype(BF16), v_ref[0, hd, :n, :]) / l).astype(BF16)

    pl.when(i < nq_ctx)(lambda: run(lc))
    pl.when(i >= nq_ctx)(lambda: run(s_len))


def _attention(q, k, v, lc):
    b, heads, s, dq = q.shape
    dv = v.shape[3]
    tq = ROW_TILE
    hp = ATTN_HEADS_PER_STEP
    return pl.pallas_call(
        functools.partial(_attn_kernel, lc=lc, nq_ctx=lc // tq),
        grid=(b, heads // hp, s // tq),
        in_specs=[pl.BlockSpec((1, hp, tq, dq), lambda bb, h, i: (bb, h, i, 0)),
                  pl.BlockSpec((1, hp, s, dq), lambda bb, h, i: (bb, h, 0, 0)),
                  pl.BlockSpec((1, hp, s, dv), lambda bb, h, i: (bb, h, 0, 0))],
        out_specs=pl.BlockSpec((1, tq, hp * dv), lambda bb, h, i: (bb, i, h)),
        out_shape=jax.ShapeDtypeStruct((b, s, heads * dv), BF16),
        compiler_params=_cparams("parallel", "parallel", "parallel"),
        name="mla_attn",
    )(q, k, v)


def _mla_out_kernel(x_ref, m_ref, o_ref, w_ref, out_ref, *, nctx):
    seg = (pl.program_id(1) >= nctx).astype(jnp.int32)
    out_ref[0] = x_ref[0] + m_ref[0, seg, 2:3, :] * _bdot(o_ref[0], w_ref[...])


def _mla_out(x, modsel, o, w, lc):
    b, s, d = x.shape
    tm = ROW_TILE
    return pl.pallas_call(
        functools.partial(_mla_out_kernel, nctx=lc // tm),
        grid=(b, s // tm),
        in_specs=[_row_spec(tm, d), _mod_spec(d), _row_spec(tm, o.shape[2]), _full_spec(w.shape)],
        out_specs=_row_spec(tm, d),
        out_shape=jax.ShapeDtypeStruct((b, s, d), F32),
        compiler_params=_cparams("parallel", "parallel"),
        name="mla_out",
    )(x, modsel, o, w)


def _router_kernel(x_ref, m_ref, g_ref, rwt_ref, rb_ref, hx_ref, meta_ref, counts_ref, run_ref, *, nctx):
    seg = (pl.program_id(1) >= nctx).astype(jnp.int32)
    tm, dm = x_ref.shape[1], x_ref.shape[2]

    @pl.when((pl.program_id(0) == 0) & (pl.program_id(1) == 0))
    def _():
        run_ref[...] = jnp.zeros_like(run_ref)

    h = _modulate(x_ref[0], g_ref[...], m_ref[0, seg, 3:4, :], m_ref[0, seg, 4:5, :])
    hx_ref[0, :, :dm] = h
    logits = lax.dot_general(rwt_ref[...], h, (((1,), (1,)), ((), ())), precision=HIGHEST,
                             preferred_element_type=F32)
    scores = _sigmoid(logits)
    sel = scores + rb_ref[...]
    sel_r = [sel[e:e + 1, :] for e in range(N_EXPERTS)]
    sc_r = [scores[e:e + 1, :] for e in range(N_EXPERTS)]

    grp = []
    for g in range(N_GROUPS):
        a, b, c, d = sel_r[GROUP_SIZE * g:GROUP_SIZE * (g + 1)]
        hi1, lo1, hi2, lo2 = jnp.maximum(a, b), jnp.minimum(a, b), jnp.maximum(c, d), jnp.minimum(c, d)
        grp.append(jnp.maximum(hi1, hi2) + jnp.maximum(jnp.minimum(hi1, hi2), jnp.maximum(lo1, lo2)))
    best = jnp.zeros((1, tm), jnp.int32)
    cur = grp[0]
    for g in range(1, N_GROUPS):
        upd = grp[g] > cur
        best = jnp.where(upd, g, best)
        cur = jnp.where(upd, grp[g], cur)

    def in_group(rows, j):
        out = rows[j]
        for g in range(1, N_GROUPS):
            out = jnp.where(best == g, rows[GROUP_SIZE * g + j], out)
        return out

    v = [in_group(sel_r, j) for j in range(GROUP_SIZE)]
    sc = [in_group(sc_r, j) for j in range(GROUP_SIZE)]

    def first_argmax(vals):
        idx = jnp.zeros((1, tm), jnp.int32)
        top = vals[0]
        for j in range(1, GROUP_SIZE):
            upd = vals[j] > top
            idx = jnp.where(upd, j, idx)
            top = jnp.where(upd, vals[j], top)
        return idx

    i1 = first_argmax(v)
    i2 = first_argmax([jnp.where(i1 == j, -jnp.inf, v[j]) for j in range(GROUP_SIZE)])
    w1 = sum(jnp.where(i1 == j, sc[j], 0.0) for j in range(GROUP_SIZE))
    w2 = sum(jnp.where(i2 == j, sc[j], 0.0) for j in range(GROUP_SIZE))
    den = w1 + w2
    first_lo = i1 < i2
    lo = jnp.minimum(i1, i2)
    hi = jnp.maximum(i1, i2)
    w_lo = jnp.where(first_lo, w1, w2) / den
    w_hi = jnp.where(first_lo, w2, w1) / den
    pair = jnp.where(lo == 0, hi - 1, jnp.where(lo == 1, hi + 1, 5))
    cls = best * 6 + pair

    crow = lax.broadcasted_iota(jnp.int32, (CLASS_ROWS, tm), 0)
    onehot = crow == cls
    tri = (lax.broadcasted_iota(jnp.int32, (tm, tm), 0) <= lax.broadcasted_iota(jnp.int32, (tm, tm), 1))
    prefix = _bdot(jnp.where(onehot, 1.0, 0.0).astype(BF16), jnp.where(tri, 1.0, 0.0).astype(BF16))
    run = run_ref[...]
    rank = jnp.sum(jnp.where(onehot, prefix - 1.0 + run[:, 0:1], 0.0), axis=0, keepdims=True)
    run = run + prefix[:, tm - 1:tm]
    run_ref[...] = run
    counts_ref[...] = run

    r8 = lax.broadcasted_iota(jnp.int32, (8, tm), 0)
    meta_ref[...] = jnp.where(r8 == 0, cls.astype(F32), jnp.where(r8 == 1, rank, 0.0))
    rl = lax.broadcasted_iota(jnp.int32, (LANES, tm), 0)
    wt = jnp.where(rl == 0, w_lo, jnp.where(rl == 1, w_hi, 0.0))
    hx_ref[0, :, dm:] = wt.T


def _router(x, modsel, g, router_w, router_bias, lc):
    b, s, d = x.shape
    tm = ROW_TILE
    nt = s // tm
    return pl.pallas_call(
        functools.partial(_router_kernel, nctx=lc // tm),
        grid=(b, nt),
        in_specs=[_row_spec(tm, d), _mod_spec(d), _full_spec((1, d)), _full_spec((N_EXPERTS, d)),
                  _full_spec((N_EXPERTS, 1))],
        out_specs=[_row_spec(tm, d + LANES), pl.BlockSpec((8, tm), lambda bb, i: (0, bb * nt + i)),
                   _full_spec((CLASS_ROWS, LANES))],
        out_shape=[jax.ShapeDtypeStruct((b, s, d + LANES), F32), jax.ShapeDtypeStruct((8, b * s), F32),
                   jax.ShapeDtypeStruct((CLASS_ROWS, LANES), F32)],
        scratch_shapes=[pltpu.VMEM((CLASS_ROWS, LANES), F32)],
        compiler_params=_cparams("arbitrary", "arbitrary"),
        name="moe_router",
    )(x, modsel, g.reshape(1, d), router_w.T, router_bias.reshape(N_EXPERTS, 1))


def _row_copy(src, dst, sem):
    return pltpu.make_async_copy(src, dst, sem)


def _split_row(p):
    return lax.shift_right_logical(p, SUBLANES.bit_length() - 1), lax.bitwise_and(p, SUBLANES - 1)


def _zero_fill(zpos_ref, zlen_ref, zbuf, xs_ref, sem, wait):
    for c in range(zpos_ref.shape[0]):
        units = zlen_ref[c]
        done = 0
        bit = zbuf.shape[0]
        while bit >= 1:
            take = units & bit

            @pl.when(take != 0)
            def _(bit=bit, done=done):
                start = _split_row(zpos_ref[c])[0] + done
                cp = _row_copy(zbuf.at[pl.ds(0, bit)], xs_ref.at[pl.ds(start, bit)], sem)
                cp.wait() if wait else cp.start()

            done = done + take
            bit //= 2


def _dispatch_kernel(zpos_ref, zlen_ref, pos_ref, hx_ref, xs_ref, zbuf, sem):
    groups = hx_ref.shape[0]

    @pl.when(pl.program_id(0) == 0)
    def _():
        zbuf[...] = jnp.zeros_like(zbuf)
        _zero_fill(zpos_ref, zlen_ref, zbuf, xs_ref, sem, False)
        _zero_fill(zpos_ref, zlen_ref, zbuf, xs_ref, sem, True)

    def start(i, carry):
        for k in range(SUBLANES):
            grp, sub = _split_row(pos_ref[0, 0, i * SUBLANES + k])
            _row_copy(hx_ref.at[i, pl.ds(k, 1)], xs_ref.at[grp, pl.ds(sub, 1)], sem).start()
        return carry

    lax.fori_loop(0, groups, start, 0)
    _row_copy(hx_ref, xs_ref.at[pl.ds(0, groups)], sem).wait()


def _dispatch(hx, pos, zpos, zlen, n_rows):
    n, dx = hx.shape
    tm = DISPATCH_TILE
    return pl.pallas_call(
        _dispatch_kernel,
        grid_spec=pltpu.PrefetchScalarGridSpec(
            num_scalar_prefetch=2,
            grid=(n // tm,),
            in_specs=[pl.BlockSpec((1, 1, tm), lambda i, zp, zl: (i, 0, 0), memory_space=pltpu.SMEM),
                      pl.BlockSpec((tm // SUBLANES, SUBLANES, dx), lambda i, zp, zl: (i, 0, 0))],
            out_specs=pl.BlockSpec(memory_space=pl.ANY),
            scratch_shapes=[pltpu.VMEM((EXP_TILE // SUBLANES, SUBLANES, dx), F32), pltpu.SemaphoreType.DMA(())],
        ),
        out_shape=jax.ShapeDtypeStruct((n_rows // SUBLANES, SUBLANES, dx), F32),
        compiler_params=_cparams("arbitrary"),
        name="moe_dispatch",
    )(zpos, zlen, pos.reshape(n // tm, 1, tm), hx.reshape(n // SUBLANES, SUBLANES, dx)).reshape(n_rows, dx)


def _experts_kernel(tblk_ref, elo_ref, ehi_ref, valid_ref, xs_ref, w1l, w3l, w2l, w1h, w3h, w2h, ys_ref):
    j = pl.program_id(0)
    d = ys_ref.shape[1]

    @pl.when(valid_ref[j] == 1)
    def _():
        x = xs_ref[:, :d].astype(BF16)

        def hidden(w1, w3, w):
            a = _bdot(x, w1[0])
            return ((a * _sigmoid(a)) * _bdot(x, w3[0]) * w).astype(BF16)

        ys_ref[...] = (_bdot(hidden(w1l, w3l, xs_ref[:, d:d + 1]), w2l[0])
                       + _bdot(hidden(w1h, w3h, xs_ref[:, d + 1:d + 2]), w2h[0]))

    @pl.when(valid_ref[j] == 0)
    def _():
        ys_ref[...] = jnp.zeros_like(ys_ref)


def _experts(xs, tblk, elo, ehi, valid, w1, w3, w2, n_tiles):
    dx = xs.shape[1]
    ne, d, f = w1.shape
    tm = EXP_TILE
    wlo = lambda shape: pl.BlockSpec(shape, lambda j, tb, lo, hi, va: (lo[j], 0, 0))
    whi = lambda shape: pl.BlockSpec(shape, lambda j, tb, lo, hi, va: (hi[j], 0, 0))
    return pl.pallas_call(
        _experts_kernel,
        grid_spec=pltpu.PrefetchScalarGridSpec(
            num_scalar_prefetch=4,
            grid=(n_tiles,),
            in_specs=[pl.BlockSpec((tm, dx), lambda j, tb, lo, hi, va: (tb[j], 0)),
                      wlo((1, d, f)), wlo((1, d, f)), wlo((1, f, d)), whi((1, d, f)), whi((1, d, f)), whi((1, f, d))],
            out_specs=pl.BlockSpec((tm, d), lambda j, tb, lo, hi, va: (j, 0)),
        ),
        out_shape=jax.ShapeDtypeStruct((n_tiles * tm, d), F32),
        compiler_params=_cparams("arbitrary"),
        name="moe_experts",
    )(tblk, elo, ehi, valid, xs, w1, w3, w2, w1, w3, w2)


def _combine_kernel(pos_ref, posn_ref, x_ref, m_ref, ys_ref, o_ref, buf, sem, *, nt, nctx, skip):
    t = pl.program_id(0)
    n = pl.num_programs(0)
    tm = x_ref.shape[1]
    slot = t % 2

    def gather(p_ref, sl):
        def body(i, carry):
            for k in range(SUBLANES):
                grp, sub = _split_row(p_ref[0, 0, i * SUBLANES + k])
                _row_copy(ys_ref.at[grp, pl.ds(sub, 1)], buf.at[sl, i, pl.ds(k, 1)], sem.at[sl]).start()
            return carry
        lax.fori_loop(0, tm // SUBLANES, body, 0)

    @pl.when(t == 0)
    def _():
        gather(pos_ref, 0)

    @pl.when(t + 1 < n)
    def _():
        gather(posn_ref, 1 - slot)

    _row_copy(ys_ref.at[pl.ds(0, tm // SUBLANES)], buf.at[slot], sem.at[slot]).wait()
    seg = ((t % nt) + skip >= nctx).astype(jnp.int32)
    o_ref[0] = x_ref[0] + m_ref[0, seg, 5:6, :] * buf[slot].reshape(tm, x_ref.shape[2])


def _combine(x, modsel, ys, pos, lc, latent_only):
    b, s, d = x.shape
    tm = ROW_TILE
    nctx = lc // tm
    skip = nctx if latent_only else 0
    nt = s // tm - skip
    n = b * nt
    pos3 = pos.reshape(b, s // tm, tm)[:, skip:].reshape(n, 1, tm)
    return pl.pallas_call(
        functools.partial(_combine_kernel, nt=nt, nctx=nctx, skip=skip),
        grid=(n,),
        in_specs=[pl.BlockSpec((1, 1, tm), lambda t: (t, 0, 0), memory_space=pltpu.SMEM),
                  pl.BlockSpec((1, 1, tm), lambda t: (jnp.minimum(t + 1, n - 1), 0, 0), memory_space=pltpu.SMEM),
                  pl.BlockSpec((1, tm, d), lambda t: (t // nt, t % nt + skip, 0)),
                  pl.BlockSpec((1, 2, 6, d), lambda t: (t // nt, 0, 0, 0)),
                  pl.BlockSpec(memory_space=pl.ANY)],
        out_specs=pl.BlockSpec((1, tm, d), lambda t: (t // nt, t % nt, 0)),
        out_shape=jax.ShapeDtypeStruct((b, nt * tm, d), F32),
        scratch_shapes=[pltpu.VMEM((2, tm // SUBLANES, SUBLANES, d), F32), pltpu.SemaphoreType.DMA((2,))],
        compiler_params=_cparams("arbitrary"),
        name="moe_combine",
    )(pos3, pos3, x, modsel, ys.reshape(ys.shape[0] // SUBLANES, SUBLANES, d))


def _class_experts():
    lo, hi = [], []
    for g in range(N_GROUPS):
        for a in range(GROUP_SIZE):
            for c in range(a + 1, GROUP_SIZE):
                lo.append(GROUP_SIZE * g + a)
                hi.append(GROUP_SIZE * g + c)
    return jnp.array(lo, jnp.int32), jnp.array(hi, jnp.int32)


def _moe(x, modsel, g, router_w, router_bias, w1, w3, w2, lc, latent_only):
    b, s, d = x.shape
    n = b * s
    tm = EXP_TILE
    hx, meta, counts = _router(x, modsel, g, router_w, router_bias, lc)

    cnt = counts[:N_CLASSES, 0].astype(jnp.int32)
    padded = (cnt + tm - 1) // tm * tm
    ends = jnp.cumsum(padded)
    off = ends - padded
    pos = off[meta[0].astype(jnp.int32)] + meta[1].astype(jnp.int32)
    n_tiles = n // tm + N_CLASSES
    start = jnp.arange(n_tiles, dtype=jnp.int32) * tm
    valid = start < ends[-1]
    last = ends[-1] // tm - 1
    tblk = jnp.minimum(jnp.arange(n_tiles, dtype=jnp.int32), last)
    tcls = jnp.minimum(jnp.sum(ends[None, :] <= (tblk * tm)[:, None], axis=1), N_CLASSES - 1).astype(jnp.int32)
    lo_tab, hi_tab = _class_experts()

    n_rows = n_tiles * tm
    pad0 = (off + cnt) // SUBLANES * SUBLANES
    tail = ends[-1] + jnp.arange(N_CLASSES, dtype=jnp.int32) * tm
    zpos = jnp.concatenate([pad0, jnp.minimum(tail, n_rows - tm)]).astype(jnp.int32)
    zlen = jnp.concatenate([(ends - pad0) // SUBLANES, jnp.where(tail < n_rows, tm // SUBLANES, 0)]).astype(jnp.int32)
    xs = _dispatch(hx.reshape(n, d + LANES), pos, zpos, zlen, n_rows)
    ys = _experts(xs, tblk, lo_tab[tcls], hi_tab[tcls], valid.astype(jnp.int32), w1, w3, w2, n_tiles)
    return _combine(x, modsel, ys, pos, lc, latent_only)


def _rope_tables(lc, l):
    half = MLA_ROPE // 4
    freqs = ROPE_BASE ** (-jnp.arange(half, dtype=F32) / half)
    pos = jnp.arange(l)
    pos_row = (pos // GRID_W).astype(F32)
    pos_col = (pos % GRID_W).astype(F32)
    ang = jnp.concatenate([pos_row[:, None] * freqs, pos_row[:, None] * freqs,
                           pos_col[:, None] * freqs, pos_col[:, None] * freqs], axis=1)
    pad = jnp.zeros((l, 128 - MLA_ROPE), F32)
    cos = jnp.concatenate([jnp.concatenate([jnp.ones((lc, MLA_ROPE), F32), jnp.zeros((lc, 128 - MLA_ROPE), F32)], axis=1),
                           jnp.concatenate([jnp.cos(ang), pad], axis=1)], axis=0)
    sin = jnp.concatenate([jnp.zeros((lc, 128), F32), jnp.concatenate([jnp.sin(ang), pad], axis=1)], axis=0)
    r = jnp.zeros((128, 128), F32)
    idx = jnp.arange(half)
    for base in (0, MLA_ROPE // 2):
        r = r.at[base + half + idx, base + idx].set(-1.0)
        r = r.at[base + idx, base + half + idx].set(1.0)
    return cos, sin, r.astype(BF16)


def _pad_heads(w, heads):
    k = w.shape[0]
    w = w.reshape(k, heads, MLA_NOPE + MLA_ROPE)
    w = jnp.concatenate([w, jnp.zeros((k, heads, MLA_QK - MLA_NOPE - MLA_ROPE), w.dtype)], axis=2)
    return w.reshape(k, heads * MLA_QK)


def kernel(x, c, ctx, c_ctx, mod_w, mod_b, norm1_g, norm2_g, hyb_in_w, hyb_out_w, s5_a_re, s5_a_im, s5_log_dt, s5_b_re, s5_b_im, s5_c_re, s5_c_im, s5_d, s5_glu_w, s5_glu_b, gm_ln_g, gm_ln_b, gm_ws, gm_bs, mla_in_w, mla_q_norm_g, mla_kv_norm_g, mla_wq_b, mla_wkv_b, mla_qn_g, mla_qr_g, mla_kn_g, mla_kr_g, mla_out_w, router_w, router_bias, moe_w1, moe_w3, moe_w2):
    b, l, d = x.shape
    lc = ctx.shape[1]
    depth = mod_w.shape[0]
    heads = mla_out_w.shape[1] // MLA_V
    assert lc % ROW_TILE == 0 and l % ROW_TILE == 0 and b % S5_BATCH_TILE == 0

    xs = jnp.concatenate([ctx, x], axis=1)

    bp = -(-(b + 1) // 8) * 8
    cond = jnp.concatenate([c, c_ctx[None], jnp.zeros((bp - b - 1, d), F32)], axis=0)
    mods = _adaln(cond, mod_w, mod_b)

    cos, sin, rot = _rope_tables(lc, l)

    for layer in range(depth):
        i = layer // 2
        m = mods[layer].reshape(bp, 6, d)
        modsel = jnp.stack([jnp.broadcast_to(m[b][None], (b, 6, d)), m[:b]], axis=1)

        if layer % 2 == 0:
            u, zg = _hyb_in(xs, modsel, norm1_g[layer], hyb_in_w[i].astype(BF16), lc)
            ab_re, ab_im, bb_re, bb_im = _s5_discretize(s5_a_re[i], s5_a_im[i], s5_log_dt[i], s5_b_re[i], s5_b_im[i])
            ys = []
            for dr in range(2):
                mats = _s5_matrices(ab_re[dr], ab_im[dr], bb_re[dr], bb_im[dr], s5_c_re[i, dr], s5_c_im[i, dr])
                ys.append(_s5_scan(u, mats, lc, reverse=bool(dr)))
            hd = (d // 2) // GM_HEADS
            bsx = jnp.repeat(gm_bs[i].T, hd, axis=1)
            xs = _hyb_mid(xs, modsel, u, ys[0], ys[1], zg, s5_d[i], s5_glu_w[i].astype(BF16), s5_glu_b[i],
                          gm_ln_g[i], gm_ln_b[i], gm_ws[i].astype(BF16), bsx, hyb_out_w[i].astype(BF16), lc)
        else:
            q_lora = mla_q_norm_g.shape[1]
            kv_lora = mla_kv_norm_g.shape[1]
            win = jnp.concatenate([mla_in_w[i], jnp.zeros((d, 128 - MLA_ROPE), F32)], axis=1).astype(BF16)
            wq = _pad_heads(mla_wq_b[i], heads).astype(BF16)
            q, k, v = _mla_in(xs, modsel, norm1_g[layer], win, mla_q_norm_g[i], mla_kv_norm_g[i], wq,
                              mla_wkv_b[i].astype(BF16), mla_qn_g[i], mla_qr_g[i], mla_kn_g[i], mla_kr_g[i],
                              rot, cos, sin, lc, heads)
            o = _attention(q, k, v, lc)
            xs = _mla_out(xs, modsel, o, mla_out_w[i].astype(BF16), lc)

        xs = _moe(xs, modsel, norm2_g[layer], router_w, router_bias, moe_w1[layer].astype(BF16),
                  moe_w3[layer].astype(BF16), moe_w2[layer].astype(BF16), lc, latent_only=layer == depth - 1)

    return xs
```

```python
import functools
import math

import jax
import jax.numpy as jnp
from jax import lax
from jax.experimental import pallas as pl
from jax.experimental.pallas import tpu as pltpu

F32 = jnp.float32
BF16 = jnp.bfloat16
HIGHEST = lax.Precision.HIGHEST

NORM_EPS = 1e-6
ROPE_BASE = 10000.0
GRID_W = 64

S5_GROUP_CH = 16
S5_STATE = 64
S5_BATCH_TILE = 8
S5_CHUNK = 128
S5_COLS = 512

GM_HEADS = 8
GM_CHUNK = 128

MLA_NOPE = 128
MLA_ROPE = 64
MLA_V = 128
MLA_QK = 256

N_EXPERTS = 16
N_GROUPS = 4
GROUP_SIZE = 4

ROW_TILE = 256
ATTN_HEADS_PER_STEP = 8
N_CLASSES = 24
CLASS_ROWS = 32
EXP_TILE = 512
DISPATCH_TILE = 512
LANES = 128
SUBLANES = 8
VMEM_LIMIT = 56 * 1024 * 1024


def _cparams(*sem):
    return pltpu.CompilerParams(dimension_semantics=sem, vmem_limit_bytes=VMEM_LIMIT)


def _sigmoid(x):
    return 1.0 / (1.0 + jnp.exp(-x))


def _gelu_tanh(x):
    c = math.sqrt(2.0 / math.pi)
    return 0.5 * x * (1.0 + jnp.tanh(c * (x + 0.044715 * (x * x * x))))


def _rms(x, n=None):
    n = x.shape[-1] if n is None else n
    return x * lax.rsqrt(jnp.sum(x * x, axis=-1, keepdims=True) * (1.0 / n) + NORM_EPS)


def _modulate(x, g, shift, scale):
    return (_rms(x) * g) * (1.0 + scale) + shift


def _bdot(a, b):
    return jnp.dot(a, b, preferred_element_type=F32)


def _adaln_kernel(c_ref, w_ref, b_ref, o_ref):
    c = c_ref[...]
    s = c * _sigmoid(c)
    o_ref[0] = jnp.dot(s, w_ref[0], precision=HIGHEST, preferred_element_type=F32) + b_ref[0]


def _adaln(cond, mod_w, mod_b):
    depth, d, n = mod_w.shape
    bp = cond.shape[0]
    tn = n // 4
    return pl.pallas_call(
        _adaln_kernel,
        grid=(depth, n // tn),
        in_specs=[
            pl.BlockSpec((bp, d), lambda l, j: (0, 0)),
            pl.BlockSpec((1, d, tn), lambda l, j: (l, 0, j)),
            pl.BlockSpec((1, 1, tn), lambda l, j: (l, 0, j)),
        ],
        out_specs=pl.BlockSpec((1, bp, tn), lambda l, j: (l, 0, j)),
        out_shape=jax.ShapeDtypeStruct((depth, bp, n), F32),
        compiler_params=_cparams("parallel", "parallel"),
        name="adaln",
    )(cond, mod_w, mod_b.reshape(depth, 1, n))


def _mod_spec(d):
    return pl.BlockSpec((1, 2, 6, d), lambda b, i: (b, 0, 0, 0))


def _row_spec(tm, n):
    return pl.BlockSpec((1, tm, n), lambda b, i: (b, i, 0))


def _full_spec(shape):
    nd = len(shape)
    return pl.BlockSpec(shape, lambda b, i: (0,) * nd)


def _hyb_in_kernel(x_ref, m_ref, g_ref, w_ref, u_ref, zg_ref, *, nctx, s5w):
    seg = (pl.program_id(1) >= nctx).astype(jnp.int32)
    h = _modulate(x_ref[0], g_ref[...], m_ref[0, seg, 0:1, :], m_ref[0, seg, 1:2, :])
    z = _bdot(h.astype(BF16), w_ref[...])
    u_ref[0] = z[:, :s5w].astype(BF16)
    zg_ref[0] = z[:, s5w:].astype(BF16)


def _hyb_in(x, modsel, g, w, lc):
    b, s, d = x.shape
    n = w.shape[1]
    s5w = d // 2
    tm = ROW_TILE
    return pl.pallas_call(
        functools.partial(_hyb_in_kernel, nctx=lc // tm, s5w=s5w),
        grid=(b, s // tm),
        in_specs=[_row_spec(tm, d), _mod_spec(d), _full_spec((1, d)), _full_spec((d, n))],
        out_specs=[_row_spec(tm, s5w), _row_spec(tm, n - s5w)],
        out_shape=[jax.ShapeDtypeStruct((b, s, s5w), BF16), jax.ShapeDtypeStruct((b, s, n - s5w), BF16)],
        compiler_params=_cparams("parallel", "parallel"),
        name="hyb_in",
    )(x, modsel, g.reshape(1, d), w)


def _s5_disc_kernel(are_ref, aim_ref, ldt_ref, bre_ref, bim_ref, abre_ref, abim_ref, bbre_ref, bbim_ref):
    a_re = are_ref[0]
    a_im = aim_ref[0]
    dt = jnp.exp(ldt_ref[0])
    mag = jnp.exp(dt * a_re)
    ab_re = mag * jnp.cos(dt * a_im)
    ab_im = mag * jnp.sin(dt * a_im)
    nr = ab_re - 1.0
    ni = ab_im
    den = a_re * a_re + a_im * a_im
    f_re = (nr * a_re + ni * a_im) / den
    f_im = (ni * a_re - nr * a_im) / den
    abre_ref[0] = ab_re
    abim_ref[0] = ab_im
    b_re = bre_ref[0]
    b_im = bim_ref[0]
    bbre_ref[0] = f_re[None] * b_re - f_im[None] * b_im
    bbim_ref[0] = f_re[None] * b_im + f_im[None] * b_re


def _s5_discretize(a_re, a_im, log_dt, b_re, b_im):
    nd, g, n = a_re.shape
    p = b_re.shape[-1]
    bt_re = jnp.transpose(b_re, (0, 3, 1, 2))
    bt_im = jnp.transpose(b_im, (0, 3, 1, 2))
    s3 = pl.BlockSpec((1, g, n), lambda i: (i, 0, 0))
    s4 = pl.BlockSpec((1, p, g, n), lambda i: (i, 0, 0, 0))
    return pl.pallas_call(
        _s5_disc_kernel,
        grid=(nd,),
        in_specs=[s3, s3, pl.BlockSpec((1, g, 1), lambda i: (i, 0, 0)), s4, s4],
        out_specs=[s3, s3, s4, s4],
        out_shape=[jax.ShapeDtypeStruct((nd, g, n), F32)] * 2 + [jax.ShapeDtypeStruct((nd, p, g, n), F32)] * 2,
        compiler_params=_cparams("parallel"),
        name="s5_disc",
    )(a_re, a_im, log_dt.reshape(nd, g, 1), bt_re, bt_im)


def _s5_matrices(ab_re, ab_im, bb_re, bb_im, c_re, c_im):
    p, g, n = bb_re.shape
    gh = g // 2
    eye = jnp.eye(gh, dtype=F32)

    def in_half(bb, hh):
        blk = jnp.transpose(bb[:, hh * gh:(hh + 1) * gh], (1, 0, 2))
        return jnp.einsum("gpn,gk->gpkn", blk, eye).reshape(gh * p, gh * n)

    def out_half(c, hh):
        blk = c[hh * gh:(hh + 1) * gh]
        return jnp.einsum("gpn,gk->gnkp", blk, eye).reshape(gh * n, gh * p)

    bm = jnp.stack([jnp.concatenate([in_half(bb_re, hh), in_half(bb_im, hh)], axis=1) for hh in range(2)])
    cre = jnp.stack([out_half(c_re, hh) for hh in range(2)])
    cim = jnp.stack([out_half(c_im, hh) for hh in range(2)])
    return (bm.astype(BF16), ab_re.reshape(1, g * n), ab_im.reshape(1, g * n), cre.astype(BF16), cim.astype(BF16))


def _s5_scan_kernel(u_ref, bm_ref, are_ref, aim_ref, cre_ref, cim_ref, y_ref, bure, buim, hre, him, *, reverse):
    bt, t_len, w = u_ref.shape
    ns = hre.shape[1]
    half = ns // 2

    @pl.when(pl.program_id(1) == 0)
    def _():
        hre[...] = jnp.zeros_like(hre)
        him[...] = jnp.zeros_like(him)

    u_tb = jnp.swapaxes(u_ref[...].astype(F32), 0, 1).reshape(t_len * bt, w).astype(BF16)
    ys = []
    for hh in range(2):
        hs = slice(hh * half, (hh + 1) * half)
        bu = _bdot(u_tb[:, hh * (w // 2):(hh + 1) * (w // 2)], bm_ref[hh])
        bure[:, hs] = bu[:, :half]
        buim[:, hs] = bu[:, half:]
        for c0 in range(hh * half, (hh + 1) * half, S5_COLS):
            cs = slice(c0, c0 + S5_COLS)
            a_r = jnp.broadcast_to(are_ref[:, cs], (bt, S5_COLS))
            a_i = jnp.broadcast_to(aim_ref[:, cs], (bt, S5_COLS))
            h_r, h_i = hre[:, cs], him[:, cs]
            for t in range(t_len):
                tt = (t_len - 1 - t) if reverse else t
                rows = slice(tt * bt, (tt + 1) * bt)
                h_r, h_i = (a_r * h_r - a_i * h_i + bure[rows, cs], a_r * h_i + a_i * h_r + buim[rows, cs])
                bure[rows, cs] = h_r
                buim[rows, cs] = h_i
            hre[:, cs] = h_r
            him[:, cs] = h_i
        ys.append(_bdot(bure[:, hs].astype(BF16), cre_ref[hh]) - _bdot(buim[:, hs].astype(BF16), cim_ref[hh]))
    y_tb = jnp.concatenate(ys, axis=1).reshape(t_len, bt, w)
    y_ref[...] = jnp.swapaxes(y_tb, 0, 1).astype(BF16)


def _s5_scan(u, mats, lc, reverse):
    bm, a_r, a_i, cre, cim = mats
    b, s, w = u.shape
    bt, t_len = S5_BATCH_TILE, S5_CHUNK
    nch, nctx = s // t_len, lc // t_len
    ns = a_r.shape[1]

    def chunk(k):
        if not reverse:
            return k
        return jnp.where(k < nctx, nctx - 1 - k, nch - 1 - (k - nctx))

    blk = pl.BlockSpec((bt, t_len, w), lambda g, k: (g, chunk(k), 0))
    return pl.pallas_call(
        functools.partial(_s5_scan_kernel, reverse=reverse),
        grid=(b // bt, nch),
        in_specs=[blk, _full_spec(bm.shape), _full_spec(a_r.shape), _full_spec(a_i.shape),
                  _full_spec(cre.shape), _full_spec(cim.shape)],
        out_specs=blk,
        out_shape=jax.ShapeDtypeStruct((b, s, w), BF16),
        scratch_shapes=[pltpu.VMEM((t_len * bt, ns), F32), pltpu.VMEM((t_len * bt, ns), F32),
                        pltpu.VMEM((bt, ns), F32), pltpu.VMEM((bt, ns), F32)],
        compiler_params=_cparams("parallel", "arbitrary"),
        name="s5_bwd" if reverse else "s5_fwd",
    )(u, bm, a_r, a_i, cre, cim)


def _hyb_mid_kernel(x_ref, m_ref, u_ref, yf_ref, yb_ref, zg_ref, d_ref, gw_ref, gb_ref, lg_ref, lb_ref,
                    ws_ref, bsx_ref, ow_ref, o_ref, *, nctx):
    seg = (pl.program_id(1) >= nctx).astype(jnp.int32)
    tm = x_ref.shape[1]
    w = u_ref.shape[2]
    y = u_ref[0].astype(F32) * d_ref[...] + yf_ref[0].astype(F32) + yb_ref[0].astype(F32)
    sg = _gelu_tanh(y)
    s5o = sg * _sigmoid(_bdot(sg.astype(BF16), gw_ref[...]) + gb_ref[...])
    zg = _gelu_tanh(zg_ref[0].astype(F32))
    gu = zg[:, :w]
    v = zg[:, w:]
    mu = jnp.mean(v, axis=-1, keepdims=True)
    vc = v - mu
    var = jnp.mean(vc * vc, axis=-1, keepdims=True)
    vn = (vc * lax.rsqrt(var + NORM_EPS) * lg_ref[...] + lb_ref[...]).astype(BF16)
    lane = lax.broadcasted_iota(jnp.int32, (GM_CHUNK, 128), 1)
    low = lane < (w // GM_HEADS)
    chunks = []
    for c in range(tm // GM_CHUNK):
        vch = vn[c * GM_CHUNK:(c + 1) * GM_CHUNK]
        cols = []
        for kp in range(GM_HEADS // 2):
            vs = vch[:, kp * 128:(kp + 1) * 128]
            cols.append(jnp.where(low, _bdot(ws_ref[2 * kp], vs), _bdot(ws_ref[2 * kp + 1], vs)))
        chunks.append(jnp.concatenate(cols, axis=1) + bsx_ref[...])
    gm = gu * jnp.concatenate(chunks, axis=0)
    ycat = jnp.concatenate([s5o, gm], axis=1).astype(BF16)
    o_ref[0] = x_ref[0] + m_ref[0, seg, 2:3, :] * _bdot(ycat, ow_ref[...])


def _hyb_mid(x, modsel, u, yf, yb, zg, d, glu_w, glu_b, ln_g, ln_b, ws, bsx, out_w, lc):
    b, s, dm = x.shape
    w = u.shape[2]
    tm = ROW_TILE
    vec = _full_spec((1, w))
    return pl.pallas_call(
        functools.partial(_hyb_mid_kernel, nctx=lc // tm),
        grid=(b, s // tm),
        in_specs=[_row_spec(tm, dm), _mod_spec(dm), _row_spec(tm, w), _row_spec(tm, w), _row_spec(tm, w),
                  _row_spec(tm, 2 * w), vec, _full_spec((w, w)), vec, vec, vec,
                  _full_spec(ws.shape), _full_spec(bsx.shape), _full_spec(out_w.shape)],
        out_specs=_row_spec(tm, dm),
        out_shape=jax.ShapeDtypeStruct((b, s, dm), F32),
        compiler_params=_cparams("parallel", "parallel"),
        name="hyb_mid",
    )(x, modsel, u, yf, yb, zg, d.reshape(1, w), glu_w, glu_b.reshape(1, w), ln_g.reshape(1, w),
      ln_b.reshape(1, w), ws, bsx, out_w)


def _mla_in_kernel(x_ref, m_ref, g_ref, win_ref, qng_ref, kvng_ref, wq_ref, wkv_ref, qn_ref, qr_ref, kn_ref,
                   kr_ref, rot_ref, cos_ref, sin_ref, q_out, k_out, v_out, *, nctx, q_lora, kv_lora, heads, scale):
    seg = (pl.program_id(1) >= nctx).astype(jnp.int32)
    h = _modulate(x_ref[0], g_ref[...], m_ref[0, seg, 0:1, :], m_ref[0, seg, 1:2, :])
    z = _bdot(h.astype(BF16), win_ref[...])
    cq = z[:, :q_lora]
    ckv = z[:, q_lora:q_lora + kv_lora]
    kr = z[:, q_lora + kv_lora:]
    q = _bdot((_rms(cq) * qng_ref[...]).astype(BF16), wq_ref[...])
    kv = _bdot((_rms(ckv) * kvng_ref[...]).astype(BF16), wkv_ref[...])
    cos = cos_ref[...]
    sin = sin_ref[...]

    def rope(t):
        return t * cos + _bdot(t.astype(BF16), rot_ref[...]) * sin

    k_rope = rope(_rms(kr, MLA_ROPE) * kr_ref[...]).astype(BF16)
    for hd in range(heads):
        qh = q[:, hd * MLA_QK:(hd + 1) * MLA_QK]
        q_out[0, hd, :, :MLA_NOPE] = (_rms(qh[:, :MLA_NOPE]) * qn_ref[...] * scale).astype(BF16)
        q_out[0, hd, :, MLA_NOPE:] = (rope(_rms(qh[:, MLA_NOPE:], MLA_ROPE) * qr_ref[...]) * scale).astype(BF16)
        kvh = kv[:, hd * (MLA_NOPE + MLA_V):(hd + 1) * (MLA_NOPE + MLA_V)]
        k_out[0, hd, :, :MLA_NOPE] = (_rms(kvh[:, :MLA_NOPE]) * kn_ref[...]).astype(BF16)
        k_out[0, hd, :, MLA_NOPE:] = k_rope
        v_out[0, hd] = kvh[:, MLA_NOPE:].astype(BF16)


def _mla_in(x, modsel, g, win, qng, kvng, wq, wkv, qn_g, qr_g, kn_g, kr_g, rot, cos, sin, lc, heads):
    b, s, d = x.shape
    tm = ROW_TILE
    q_lora, kv_lora = qng.shape[0], kvng.shape[0]
    scale = (MLA_NOPE + MLA_ROPE) ** -0.5 * math.log2(math.e)
    pad = jnp.zeros((MLA_QK - MLA_NOPE - MLA_ROPE,), F32)
    vec = lambda a: _full_spec((1, a.shape[-1]))
    args = [g.reshape(1, d), win, qng.reshape(1, -1), kvng.reshape(1, -1), wq, wkv, qn_g.reshape(1, -1),
            jnp.concatenate([qr_g, pad]).reshape(1, -1), kn_g.reshape(1, -1),
            jnp.concatenate([kr_g, pad]).reshape(1, -1), rot]
    tab = pl.BlockSpec((tm, 128), lambda bb, i: (i, 0))
    head_spec = lambda n: pl.BlockSpec((1, heads, tm, n), lambda bb, i: (bb, 0, i, 0))
    return pl.pallas_call(
        functools.partial(_mla_in_kernel, nctx=lc // tm, q_lora=q_lora, kv_lora=kv_lora, heads=heads, scale=scale),
        grid=(b, s // tm),
        in_specs=[_row_spec(tm, d), _mod_spec(d)] + [_full_spec(a.shape) for a in args] + [tab, tab],
        out_specs=[head_spec(MLA_QK), head_spec(MLA_QK), head_spec(MLA_V)],
        out_shape=[jax.ShapeDtypeStruct((b, heads, s, MLA_QK), BF16), jax.ShapeDtypeStruct((b, heads, s, MLA_QK), BF16),
                   jax.ShapeDtypeStruct((b, heads, s, MLA_V), BF16)],
        compiler_params=_cparams("parallel", "parallel"),
        name="mla_in",
    )(x, modsel, *args, cos, sin)


def _attn_kernel(q_ref, k_ref, v_ref, o_ref, *, lc, nq_ctx):
    i = pl.program_id(2)
    hp, s_len, dv = v_ref.shape[1], v_ref.shape[2], v_ref.shape[3]

    def run(n):
        for hd in range(hp):
            q = q_ref[0, hd]
            s = lax.dot_general(q, k_ref[0, hd, :n, :], (((1,), (1,)), ((), ())), preferred_element_type=F32)
            p = jnp.exp2(s - jnp.max(s, axis=-1, keepdims=True))
            l = jnp.sum(p, axis=-1, keepdims=True)
            o_ref[0, :, hd * dv:(hd + 1) * dv] = (_bdot(p.astype(BF16), v_ref[0, hd, :n, :]) / l).astype(BF16)

    pl.when(i < nq_ctx)(lambda: run(lc))
    pl.when(i >= nq_ctx)(lambda: run(s_len))


def _attention(q, k, v, lc):
    b, heads, s, dq = q.shape
    dv = v.shape[3]
    tq = ROW_TILE
    hp = ATTN_HEADS_PER_STEP
    return pl.pallas_call(
        functools.partial(_attn_kernel, lc=lc, nq_ctx=lc // tq),
        grid=(b, heads // hp, s // tq),
        in_specs=[pl.BlockSpec((1, hp, tq, dq), lambda bb, h, i: (bb, h, i, 0)),
                  pl.BlockSpec((1, hp, s, dq), lambda bb, h, i: (bb, h, 0, 0)),
                  pl.BlockSpec((1, hp, s, dv), lambda bb, h, i: (bb, h, 0, 0))],
        out_specs=pl.BlockSpec((1, tq, hp * dv), lambda bb, h, i: (bb, i, h)),
        out_shape=jax.ShapeDtypeStruct((b, s, heads * dv), BF16),
        compiler_params=_cparams("parallel", "parallel", "parallel"),
        name="mla_attn",
    )(q, k, v)


def _mla_out_kernel(x_ref, m_ref, o_ref, w_ref, out_ref, *, nctx):
    seg = (pl.program_id(1) >= nctx).astype(jnp.int32)
    out_ref[0] = x_ref[0] + m_ref[0, seg, 2:3, :] * _bdot(o_ref[0], w_ref[...])


def _mla_out(x, modsel, o, w, lc):
    b, s, d = x.shape
    tm = ROW_TILE
    return pl.pallas_call(
        functools.partial(_mla_out_kernel, nctx=lc // tm),
        grid=(b, s // tm),
        in_specs=[_row_spec(tm, d), _mod_spec(d), _row_spec(tm, o.shape[2]), _full_spec(w.shape)],
        out_specs=_row_spec(tm, d),
        out_shape=jax.ShapeDtypeStruct((b, s, d), F32),
        compiler_params=_cparams("parallel", "parallel"),
        name="mla_out",
    )(x, modsel, o, w)


def _router_kernel(x_ref, m_ref, g_ref, rwt_ref, rb_ref, hx_ref, meta_ref, counts_ref, run_ref, *, nctx):
    seg = (pl.program_id(1) >= nctx).astype(jnp.int32)
    tm, dm = x_ref.shape[1], x_ref.shape[2]

    @pl.when((pl.program_id(0) == 0) & (pl.program_id(1) == 0))
    def _():
        run_ref[...] = jnp.zeros_like(run_ref)

    h = _modulate(x_ref[0], g_ref[...], m_ref[0, seg, 3:4, :], m_ref[0, seg, 4:5, :])
    hx_ref[0, :, :dm] = h
    logits = lax.dot_general(rwt_ref[...], h, (((1,), (1,)), ((), ())), precision=HIGHEST,
                             preferred_element_type=F32)
    scores = _sigmoid(logits)
    sel = scores + rb_ref[...]
    sel_r = [sel[e:e + 1, :] for e in range(N_EXPERTS)]
    sc_r = [scores[e:e + 1, :] for e in range(N_EXPERTS)]

    grp = []
    for g in range(N_GROUPS):
        a, b, c, d = sel_r[GROUP_SIZE * g:GROUP_SIZE * (g + 1)]
        hi1, lo1, hi2, lo2 = jnp.maximum(a, b), jnp.minimum(a, b), jnp.maximum(c, d), jnp.minimum(c, d)
        grp.append(jnp.maximum(hi1, hi2) + jnp.maximum(jnp.minimum(hi1, hi2), jnp.maximum(lo1, lo2)))
    best = jnp.zeros((1, tm), jnp.int32)
    cur = grp[0]
    for g in range(1, N_GROUPS):
        upd = grp[g] > cur
        best = jnp.where(upd, g, best)
        cur = jnp.where(upd, grp[g], cur)

    def in_group(rows, j):
        out = rows[j]
        for g in range(1, N_GROUPS):
            out = jnp.where(best == g, rows[GROUP_SIZE * g + j], out)
        return out

    v = [in_group(sel_r, j) for j in range(GROUP_SIZE)]
    sc = [in_group(sc_r, j) for j in range(GROUP_SIZE)]

    def first_argmax(vals):
        idx = jnp.zeros((1, tm), jnp.int32)
        top = vals[0]
        for j in range(1, GROUP_SIZE):
            upd = vals[j] > top
            idx = jnp.where(upd, j, idx)
            top = jnp.where(upd, vals[j], top)
        return idx

    i1 = first_argmax(v)
    i2 = first_argmax([jnp.where(i1 == j, -jnp.inf, v[j]) for j in range(GROUP_SIZE)])
    w1 = sum(jnp.where(i1 == j, sc[j], 0.0) for j in range(GROUP_SIZE))
    w2 = sum(jnp.where(i2 == j, sc[j], 0.0) for j in range(GROUP_SIZE))
    den = w1 + w2
    first_lo = i1 < i2
    lo = jnp.minimum(i1, i2)
    hi = jnp.maximum(i1, i2)
    w_lo = jnp.where(first_lo, w1, w2) / den
    w_hi = jnp.where(first_lo, w2, w1) / den
    pair = jnp.where(lo == 0, hi - 1, jnp.where(lo == 1, hi + 1, 5))
    cls = best * 6 + pair

    crow = lax.broadcasted_iota(jnp.int32, (CLASS_ROWS, tm), 0)
    onehot = crow == cls
    tri = (lax.broadcasted_iota(jnp.int32, (tm, tm), 0) <= lax.broadcasted_iota(jnp.int32, (tm, tm), 1))
    prefix = _bdot(jnp.where(onehot, 1.0, 0.0).astype(BF16), jnp.where(tri, 1.0, 0.0).astype(BF16))
    run = run_ref[...]
    rank = jnp.sum(jnp.where(onehot, prefix - 1.0 + run[:, 0:1], 0.0), axis=0, keepdims=True)
    run = run + prefix[:, tm - 1:tm]
    run_ref[...] = run
    counts_ref[...] = run

    r8 = lax.broadcasted_iota(jnp.int32, (8, tm), 0)
    meta_ref[...] = jnp.where(r8 == 0, cls.astype(F32), jnp.where(r8 == 1, rank, 0.0))
    rl = lax.broadcasted_iota(jnp.int32, (LANES, tm), 0)
    wt = jnp.where(rl == 0, w_lo, jnp.where(rl == 1, w_hi, 0.0))
    hx_ref[0, :, dm:] = wt.T


def _router(x, modsel, g, router_w, router_bias, lc):
    b, s, d = x.shape
    tm = ROW_TILE
    nt = s // tm
    return pl.pallas_call(
        functools.partial(_router_kernel, nctx=lc // tm),
        grid=(b, nt),
        in_specs=[_row_spec(tm, d), _mod_spec(d), _full_spec((1, d)), _full_spec((N_EXPERTS, d)),
                  _full_spec((N_EXPERTS, 1))],
        out_specs=[_row_spec(tm, d + LANES), pl.BlockSpec((8, tm), lambda bb, i: (0, bb * nt + i)),
                   _full_spec((CLASS_ROWS, LANES))],
        out_shape=[jax.ShapeDtypeStruct((b, s, d + LANES), F32), jax.ShapeDtypeStruct((8, b * s), F32),
                   jax.ShapeDtypeStruct((CLASS_ROWS, LANES), F32)],
        scratch_shapes=[pltpu.VMEM((CLASS_ROWS, LANES), F32)],
        compiler_params=_cparams("arbitrary", "arbitrary"),
        name="moe_router",
    )(x, modsel, g.reshape(1, d), router_w.T, router_bias.reshape(N_EXPERTS, 1))


def _row_copy(src, dst, sem):
    return pltpu.make_async_copy(src, dst, sem)


def _zero_fill(zpos_ref, zlen_ref, zbuf, xs_ref, sem, wait):
    for c in range(zpos_ref.shape[0]):
        units = zlen_ref[c]
        done = 0
        bit = zbuf.shape[0] // SUBLANES
        while bit >= 1:
            take = units & bit

            @pl.when(take != 0)
            def _(rows=bit * SUBLANES, done=done):
                start = zpos_ref[c] + done * SUBLANES
                cp = _row_copy(zbuf.at[pl.ds(0, rows)], xs_ref.at[pl.ds(start, rows)], sem)
                cp.wait() if wait else cp.start()

            done = done + take
            bit //= 2


def _dispatch_kernel(zpos_ref, zlen_ref, pos_ref, hx_ref, xs_ref, stage, zbuf, sem):
    tm = hx_ref.shape[0]

    @pl.when(pl.program_id(0) == 0)
    def _():
        zbuf[...] = jnp.zeros_like(zbuf)
        _zero_fill(zpos_ref, zlen_ref, zbuf, xs_ref, sem, False)
        _zero_fill(zpos_ref, zlen_ref, zbuf, xs_ref, sem, True)

    stage[...] = hx_ref[...].reshape(stage.shape)

    def start(i, carry):
        for k in range(SUBLANES):
            r = i * SUBLANES + k
            _row_copy(stage.at[r], xs_ref.at[pos_ref[0, 0, r]], sem).start()
        return carry

    lax.fori_loop(0, tm // SUBLANES, start, 0)
    _row_copy(stage, xs_ref.at[pl.ds(0, tm)], sem).wait()


def _dispatch(hx, pos, zpos, zlen, n_rows):
    n, dx = hx.shape
    tm = DISPATCH_TILE
    slab = (dx // LANES, LANES)
    return pl.pallas_call(
        _dispatch_kernel,
        grid_spec=pltpu.PrefetchScalarGridSpec(
            num_scalar_prefetch=2,
            grid=(n // tm,),
            in_specs=[pl.BlockSpec((1, 1, tm), lambda i, zp, zl: (i, 0, 0), memory_space=pltpu.SMEM),
                      pl.BlockSpec((tm, dx), lambda i, zp, zl: (i, 0))],
            out_specs=pl.BlockSpec(memory_space=pl.ANY),
            scratch_shapes=[pltpu.VMEM((tm,) + slab, F32), pltpu.VMEM((EXP_TILE,) + slab, F32),
                            pltpu.SemaphoreType.DMA(())],
        ),
        out_shape=jax.ShapeDtypeStruct((n_rows,) + slab, F32),
        compiler_params=_cparams("arbitrary"),
        name="moe_dispatch",
    )(zpos, zlen, pos.reshape(n // tm, 1, tm), hx)


def _experts_kernel(tblk_ref, elo_ref, ehi_ref, valid_ref, xs_ref, w1l, w3l, w2l, w1h, w3h, w2h, ys_ref):
    j = pl.program_id(0)
    tm = xs_ref.shape[0]
    d = ys_ref.shape[1] * ys_ref.shape[2]

    @pl.when(valid_ref[j] == 1)
    def _():
        xw = xs_ref[...].reshape(tm, xs_ref.shape[1] * xs_ref.shape[2])
        x = xw[:, :d].astype(BF16)

        def hidden(w1, w3, w):
            a = _bdot(x, w1[0])
            return ((a * _sigmoid(a)) * _bdot(x, w3[0]) * w).astype(BF16)

        y = (_bdot(hidden(w1l, w3l, xw[:, d:d + 1]), w2l[0]) + _bdot(hidden(w1h, w3h, xw[:, d + 1:d + 2]), w2h[0]))
        ys_ref[...] = y.reshape(ys_ref.shape)

    @pl.when(valid_ref[j] == 0)
    def _():
        ys_ref[...] = jnp.zeros_like(ys_ref)


def _experts(xs, tblk, elo, ehi, valid, w1, w3, w2, n_tiles):
    ne, d, f = w1.shape
    tm = EXP_TILE
    wlo = lambda shape: pl.BlockSpec(shape, lambda j, tb, lo, hi, va: (lo[j], 0, 0))
    whi = lambda shape: pl.BlockSpec(shape, lambda j, tb, lo, hi, va: (hi[j], 0, 0))
    return pl.pallas_call(
        _experts_kernel,
        grid_spec=pltpu.PrefetchScalarGridSpec(
            num_scalar_prefetch=4,
            grid=(n_tiles,),
            in_specs=[pl.BlockSpec((tm,) + xs.shape[1:], lambda j, tb, lo, hi, va: (tb[j], 0, 0)),
                      wlo((1, d, f)), wlo((1, d, f)), wlo((1, f, d)), whi((1, d, f)), whi((1, d, f)), whi((1, f, d))],
            out_specs=pl.BlockSpec((tm, d // LANES, LANES), lambda j, tb, lo, hi, va: (j, 0, 0)),
        ),
        out_shape=jax.ShapeDtypeStruct((n_tiles * tm, d // LANES, LANES), F32),
        compiler_params=_cparams("arbitrary"),
        name="moe_experts",
    )(tblk, elo, ehi, valid, xs, w1, w3, w2, w1, w3, w2)


def _combine_kernel(pos_ref, posn_ref, x_ref, m_ref, ys_ref, o_ref, buf, sem, *, nt, nctx, skip):
    t = pl.program_id(0)
    n = pl.num_programs(0)
    tm = x_ref.shape[1]
    slot = t % 2

    def gather(p_ref, sl):
        def body(i, carry):
            for k in range(SUBLANES):
                r = i * SUBLANES + k
                _row_copy(ys_ref.at[p_ref[0, 0, r]], buf.at[sl, r], sem.at[sl]).start()
            return carry
        lax.fori_loop(0, tm // SUBLANES, body, 0)

    @pl.when(t == 0)
    def _():
        gather(pos_ref, 0)

    @pl.when(t + 1 < n)
    def _():
        gather(posn_ref, 1 - slot)

    _row_copy(ys_ref.at[pl.ds(0, tm)], buf.at[slot], sem.at[slot]).wait()
    seg = ((t % nt) + skip >= nctx).astype(jnp.int32)
    o_ref[0] = x_ref[0] + m_ref[0, seg, 5:6, :] * buf[slot].reshape(tm, x_ref.shape[2])


def _combine(x, modsel, ys, pos, lc, latent_only):
    b, s, d = x.shape
    tm = ROW_TILE
    nctx = lc // tm
    skip = nctx if latent_only else 0
    nt = s // tm - skip
    n = b * nt
    pos3 = pos.reshape(b, s // tm, tm)[:, skip:].reshape(n, 1, tm)
    return pl.pallas_call(
        functools.partial(_combine_kernel, nt=nt, nctx=nctx, skip=skip),
        grid=(n,),
        in_specs=[pl.BlockSpec((1, 1, tm), lambda t: (t, 0, 0), memory_space=pltpu.SMEM),
                  pl.BlockSpec((1, 1, tm), lambda t: (jnp.minimum(t + 1, n - 1), 0, 0), memory_space=pltpu.SMEM),
                  pl.BlockSpec((1, tm, d), lambda t: (t // nt, t % nt + skip, 0)),
                  pl.BlockSpec((1, 2, 6, d), lambda t: (t // nt, 0, 0, 0)),
                  pl.BlockSpec(memory_space=pl.ANY)],
        out_specs=pl.BlockSpec((1, tm, d), lambda t: (t // nt, t % nt, 0)),
        out_shape=jax.ShapeDtypeStruct((b, nt * tm, d), F32),
        scratch_shapes=[pltpu.VMEM((2, tm) + ys.shape[1:], F32), pltpu.SemaphoreType.DMA((2,))],
        compiler_params=_cparams("arbitrary"),
        name="moe_combine",
    )(pos3, pos3, x, modsel, ys)


def _class_experts():
    lo, hi = [], []
    for g in range(N_GROUPS):
        for a in range(GROUP_SIZE):
            for c in range(a + 1, GROUP_SIZE):
                lo.append(GROUP_SIZE * g + a)
                hi.append(GROUP_SIZE * g + c)
    return jnp.array(lo, jnp.int32), jnp.array(hi, jnp.int32)


def _moe(x, modsel, g, router_w, router_bias, w1, w3, w2, lc, latent_only):
    b, s, d = x.shape
    n = b * s
    tm = EXP_TILE
    hx, meta, counts = _router(x, modsel, g, router_w, router_bias, lc)

    cnt = counts[:N_CLASSES, 0].astype(jnp.int32)
    padded = (cnt + tm - 1) // tm * tm
    ends = jnp.cumsum(padded)
    off = ends - padded
    pos = off[meta[0].astype(jnp.int32)] + meta[1].astype(jnp.int32)
    n_tiles = n // tm + N_CLASSES
    start = jnp.arange(n_tiles, dtype=jnp.int32) * tm
    valid = start < ends[-1]
    last = ends[-1] // tm - 1
    tblk = jnp.minimum(jnp.arange(n_tiles, dtype=jnp.int32), last)
    tcls = jnp.minimum(jnp.sum(ends[None, :] <= (tblk * tm)[:, None], axis=1), N_CLASSES - 1).astype(jnp.int32)
    lo_tab, hi_tab = _class_experts()

    n_rows = n_tiles * tm
    pad0 = (off + cnt) // SUBLANES * SUBLANES
    tail = ends[-1] + jnp.arange(N_CLASSES, dtype=jnp.int32) * tm
    zpos = jnp.concatenate([pad0, jnp.minimum(tail, n_rows - tm)]).astype(jnp.int32)
    zlen = jnp.concatenate([(ends - pad0) // SUBLANES, jnp.where(tail < n_rows, tm // SUBLANES, 0)]).astype(jnp.int32)
    xs = _dispatch(hx.reshape(n, d + LANES), pos, zpos, zlen, n_rows)
    ys = _experts(xs, tblk, lo_tab[tcls], hi_tab[tcls], valid.astype(jnp.int32), w1, w3, w2, n_tiles)
    return _combine(x, modsel, ys, pos, lc, latent_only)


def _rope_tables(lc, l):
    half = MLA_ROPE // 4
    freqs = ROPE_BASE ** (-jnp.arange(half, dtype=F32) / half)
    pos = jnp.arange(l)
    pos_row = (pos // GRID_W).astype(F32)
    pos_col = (pos % GRID_W).astype(F32)
    ang = jnp.concatenate([pos_row[:, None] * freqs, pos_row[:, None] * freqs,
                           pos_col[:, None] * freqs, pos_col[:, None] * freqs], axis=1)
    pad = jnp.zeros((l, 128 - MLA_ROPE), F32)
    cos = jnp.concatenate([jnp.concatenate([jnp.ones((lc, MLA_ROPE), F32), jnp.zeros((lc, 128 - MLA_ROPE), F32)], axis=1),
                           jnp.concatenate([jnp.cos(ang), pad], axis=1)], axis=0)
    sin = jnp.concatenate([jnp.zeros((lc, 128), F32), jnp.concatenate([jnp.sin(ang), pad], axis=1)], axis=0)
    r = jnp.zeros((128, 128), F32)
    idx = jnp.arange(half)
    for base in (0, MLA_ROPE // 2):
        r = r.at[base + half + idx, base + idx].set(-1.0)
        r = r.at[base + idx, base + half + idx].set(1.0)
    return cos, sin, r.astype(BF16)


def _pad_heads(w, heads):
    k = w.shape[0]
    w = w.reshape(k, heads, MLA_NOPE + MLA_ROPE)
    w = jnp.concatenate([w, jnp.zeros((k, heads, MLA_QK - MLA_NOPE - MLA_ROPE), w.dtype)], axis=2)
    return w.reshape(k, heads * MLA_QK)


def kernel(x, c, ctx, c_ctx, mod_w, mod_b, norm1_g, norm2_g, hyb_in_w, hyb_out_w, s5_a_re, s5_a_im, s5_log_dt, s5_b_re, s5_b_im, s5_c_re, s5_c_im, s5_d, s5_glu_w, s5_glu_b, gm_ln_g, gm_ln_b, gm_ws, gm_bs, mla_in_w, mla_q_norm_g, mla_kv_norm_g, mla_wq_b, mla_wkv_b, mla_qn_g, mla_qr_g, mla_kn_g, mla_kr_g, mla_out_w, router_w, router_bias, moe_w1, moe_w3, moe_w2):
    b, l, d = x.shape
    lc = ctx.shape[1]
    depth = mod_w.shape[0]
    heads = mla_out_w.shape[1] // MLA_V
    assert lc % ROW_TILE == 0 and l % ROW_TILE == 0 and b % S5_BATCH_TILE == 0

    xs = jnp.concatenate([ctx, x], axis=1)

    bp = -(-(b + 1) // 8) * 8
    cond = jnp.concatenate([c, c_ctx[None], jnp.zeros((bp - b - 1, d), F32)], axis=0)
    mods = _adaln(cond, mod_w, mod_b)

    cos, sin, rot = _rope_tables(lc, l)

    for layer in range(depth):
        i = layer // 2
        m = mods[layer].reshape(bp, 6, d)
        modsel = jnp.stack([jnp.broadcast_to(m[b][None], (b, 6, d)), m[:b]], axis=1)

        if layer % 2 == 0:
            u, zg = _hyb_in(xs, modsel, norm1_g[layer], hyb_in_w[i].astype(BF16), lc)
            ab_re, ab_im, bb_re, bb_im = _s5_discretize(s5_a_re[i], s5_a_im[i], s5_log_dt[i], s5_b_re[i], s5_b_im[i])
            ys = []
            for dr in range(2):
                mats = _s5_matrices(ab_re[dr], ab_im[dr], bb_re[dr], bb_im[dr], s5_c_re[i, dr], s5_c_im[i, dr])
                ys.append(_s5_scan(u, mats, lc, reverse=bool(dr)))
            hd = (d // 2) // GM_HEADS
            bsx = jnp.repeat(gm_bs[i].T, hd, axis=1)
            xs = _hyb_mid(xs, modsel, u, ys[0], ys[1], zg, s5_d[i], s5_glu_w[i].astype(BF16), s5_glu_b[i],
                          gm_ln_g[i], gm_ln_b[i], gm_ws[i].astype(BF16), bsx, hyb_out_w[i].astype(BF16), lc)
        else:
            q_lora = mla_q_norm_g.shape[1]
            kv_lora = mla_kv_norm_g.shape[1]
            win = jnp.concatenate([mla_in_w[i], jnp.zeros((d, 128 - MLA_ROPE), F32)], axis=1).astype(BF16)
            wq = _pad_heads(mla_wq_b[i], heads).astype(BF16)
            q, k, v = _mla_in(xs, modsel, norm1_g[layer], win, mla_q_norm_g[i], mla_kv_norm_g[i], wq,
                              mla_wkv_b[i].astype(BF16), mla_qn_g[i], mla_qr_g[i], mla_kn_g[i], mla_kr_g[i],
                              rot, cos, sin, lc, heads)
            o = _attention(q, k, v, lc)
            xs = _mla_out(xs, modsel, o, mla_out_w[i].astype(BF16), lc)

        xs = _moe(xs, modsel, norm2_g[layer], router_w, router_bias, moe_w1[layer].astype(BF16),
                  moe_w3[layer].astype(BF16), moe_w2[layer].astype(BF16), lc, latent_only=layer == depth - 1)

    return xs
```

```python
import functools
import math

import jax
import jax.numpy as jnp
from jax import lax
from jax.experimental import pallas as pl
from jax.experimental.pallas import tpu as pltpu

F32 = jnp.float32
BF16 = jnp.bfloat16
HIGHEST = lax.Precision.HIGHEST

NORM_EPS = 1e-6
ROPE_BASE = 10000.0
GRID_W = 64

S5_GROUP_CH = 16
S5_STATE = 64
S5_BATCH_TILE = 8
S5_CHUNK = 128
S5_COLS = 512

GM_HEADS = 8
GM_CHUNK = 128

MLA_NOPE = 128
MLA_ROPE = 64
MLA_V = 128
MLA_QK = 256

N_EXPERTS = 16
N_GROUPS = 4
GROUP_SIZE = 4

ROW_TILE = 256
ATTN_HEADS_PER_STEP = 8
ROUTER_TILE = 768
N_CLASSES = 24
CLASS_ROWS = 32
EXP_TILE = 512
DISPATCH_TILE = 512
LANES = 128
SUBLANES = 8
VMEM_LIMIT = 56 * 1024 * 1024


def _cparams(*sem):
    return pltpu.CompilerParams(dimension_semantics=sem, vmem_limit_bytes=VMEM_LIMIT)


def _sigmoid(x):
    return 1.0 / (1.0 + jnp.exp(-x))


def _gelu_tanh(x):
    c = math.sqrt(2.0 / math.pi)
    return 0.5 * x * (1.0 + jnp.tanh(c * (x + 0.044715 * (x * x * x))))


def _rms(x, n=None):
    n = x.shape[-1] if n is None else n
    return x * lax.rsqrt(jnp.sum(x * x, axis=-1, keepdims=True) * (1.0 / n) + NORM_EPS)


def _modulate(x, g, shift, scale):
    return (_rms(x) * g) * (1.0 + scale) + shift


def _bdot(a, b):
    return jnp.dot(a, b, preferred_element_type=F32)


def _adaln_kernel(c_ref, w_ref, b_ref, o_ref):
    c = c_ref[...]
    s = c * _sigmoid(c)
    o_ref[0] = jnp.dot(s, w_ref[0], precision=HIGHEST, preferred_element_type=F32) + b_ref[0]


def _adaln(cond, mod_w, mod_b):
    depth, d, n = mod_w.shape
    bp = cond.shape[0]
    tn = n // 4
    return pl.pallas_call(
        _adaln_kernel,
        grid=(depth, n // tn),
        in_specs=[
            pl.BlockSpec((bp, d), lambda l, j: (0, 0)),
            pl.BlockSpec((1, d, tn), lambda l, j: (l, 0, j)),
            pl.BlockSpec((1, 1, tn), lambda l, j: (l, 0, j)),
        ],
        out_specs=pl.BlockSpec((1, bp, tn), lambda l, j: (l, 0, j)),
        out_shape=jax.ShapeDtypeStruct((depth, bp, n), F32),
        compiler_params=_cparams("parallel", "parallel"),
        name="adaln",
    )(cond, mod_w, mod_b.reshape(depth, 1, n))


def _mod_spec(d):
    return pl.BlockSpec((1, 2, 6, d), lambda b, i: (b, 0, 0, 0))


def _row_spec(tm, n):
    return pl.BlockSpec((1, tm, n), lambda b, i: (b, i, 0))


def _full_spec(shape):
    nd = len(shape)
    return pl.BlockSpec(shape, lambda b, i: (0,) * nd)


def _hyb_in_kernel(x_ref, m_ref, g_ref, w_ref, u_ref, zg_ref, *, nctx, s5w):
    seg = (pl.program_id(1) >= nctx).astype(jnp.int32)
    h = _modulate(x_ref[0], g_ref[...], m_ref[0, seg, 0:1, :], m_ref[0, seg, 1:2, :])
    z = _bdot(h.astype(BF16), w_ref[...])
    u_ref[0] = z[:, :s5w].astype(BF16)
    zg_ref[0] = z[:, s5w:].astype(BF16)


def _hyb_in(x, modsel, g, w, lc):
    b, s, d = x.shape
    n = w.shape[1]
    s5w = d // 2
    tm = ROW_TILE
    return pl.pallas_call(
        functools.partial(_hyb_in_kernel, nctx=lc // tm, s5w=s5w),
        grid=(b, s // tm),
        in_specs=[_row_spec(tm, d), _mod_spec(d), _full_spec((1, d)), _full_spec((d, n))],
        out_specs=[_row_spec(tm, s5w), _row_spec(tm, n - s5w)],
        out_shape=[jax.ShapeDtypeStruct((b, s, s5w), BF16), jax.ShapeDtypeStruct((b, s, n - s5w), BF16)],
        compiler_params=_cparams("parallel", "parallel"),
        name="hyb_in",
    )(x, modsel, g.reshape(1, d), w)


def _s5_disc_kernel(are_ref, aim_ref, ldt_ref, bre_ref, bim_ref, abre_ref, abim_ref, bbre_ref, bbim_ref):
    a_re = are_ref[0]
    a_im = aim_ref[0]
    dt = jnp.exp(ldt_ref[0])
    mag = jnp.exp(dt * a_re)
    ab_re = mag * jnp.cos(dt * a_im)
    ab_im = mag * jnp.sin(dt * a_im)
    nr = ab_re - 1.0
    ni = ab_im
    den = a_re * a_re + a_im * a_im
    f_re = (nr * a_re + ni * a_im) / den
    f_im = (ni * a_re - nr * a_im) / den
    abre_ref[0] = ab_re
    abim_ref[0] = ab_im
    b_re = bre_ref[0]
    b_im = bim_ref[0]
    bbre_ref[0] = f_re[None] * b_re - f_im[None] * b_im
    bbim_ref[0] = f_re[None] * b_im + f_im[None] * b_re


def _s5_discretize(a_re, a_im, log_dt, b_re, b_im):
    nd, g, n = a_re.shape
    p = b_re.shape[-1]
    bt_re = jnp.transpose(b_re, (0, 3, 1, 2))
    bt_im = jnp.transpose(b_im, (0, 3, 1, 2))
    s3 = pl.BlockSpec((1, g, n), lambda i: (i, 0, 0))
    s4 = pl.BlockSpec((1, p, g, n), lambda i: (i, 0, 0, 0))
    return pl.pallas_call(
        _s5_disc_kernel,
        grid=(nd,),
        in_specs=[s3, s3, pl.BlockSpec((1, g, 1), lambda i: (i, 0, 0)), s4, s4],
        out_specs=[s3, s3, s4, s4],
        out_shape=[jax.ShapeDtypeStruct((nd, g, n), F32)] * 2 + [jax.ShapeDtypeStruct((nd, p, g, n), F32)] * 2,
        compiler_params=_cparams("parallel"),
        name="s5_disc",
    )(a_re, a_im, log_dt.reshape(nd, g, 1), bt_re, bt_im)


def _s5_matrices(ab_re, ab_im, bb_re, bb_im, c_re, c_im):
    p, g, n = bb_re.shape
    gh = g // 2
    eye = jnp.eye(gh, dtype=F32)

    def in_half(bb, hh):
        blk = jnp.transpose(bb[:, hh * gh:(hh + 1) * gh], (1, 0, 2))
        return jnp.einsum("gpn,gk->gpkn", blk, eye).reshape(gh * p, gh * n)

    def out_half(c, hh):
        blk = c[hh * gh:(hh + 1) * gh]
        return jnp.einsum("gpn,gk->gnkp", blk, eye).reshape(gh * n, gh * p)

    bm = jnp.stack([jnp.concatenate([in_half(bb_re, hh), in_half(bb_im, hh)], axis=1) for hh in range(2)])
    cre = jnp.stack([out_half(c_re, hh) for hh in range(2)])
    cim = jnp.stack([out_half(c_im, hh) for hh in range(2)])
    return (bm.astype(BF16), ab_re.reshape(1, g * n), ab_im.reshape(1, g * n), cre.astype(BF16), cim.astype(BF16))


def _s5_scan_kernel(u_ref, bm_ref, are_ref, aim_ref, cre_ref, cim_ref, y_ref, bure, buim, hre, him, *, reverse):
    bt, t_len, w = u_ref.shape
    ns = hre.shape[1]
    half = ns // 2

    @pl.when(pl.program_id(1) == 0)
    def _():
        hre[...] = jnp.zeros_like(hre)
        him[...] = jnp.zeros_like(him)

    u_tb = jnp.swapaxes(u_ref[...].astype(F32), 0, 1).reshape(t_len * bt, w).astype(BF16)
    ys = []
    for hh in range(2):
        hs = slice(hh * half, (hh + 1) * half)
        bu = _bdot(u_tb[:, hh * (w // 2):(hh + 1) * (w // 2)], bm_ref[hh])
        bure[:, hs] = bu[:, :half]
        buim[:, hs] = bu[:, half:]
        for c0 in range(hh * half, (hh + 1) * half, S5_COLS):
            cs = slice(c0, c0 + S5_COLS)
            a_r = jnp.broadcast_to(are_ref[:, cs], (bt, S5_COLS))
            a_i = jnp.broadcast_to(aim_ref[:, cs], (bt, S5_COLS))
            h_r, h_i = hre[:, cs], him[:, cs]
            for t in range(t_len):
                tt = (t_len - 1 - t) if reverse else t
                rows = slice(tt * bt, (tt + 1) * bt)
                h_r, h_i = (a_r * h_r - a_i * h_i + bure[rows, cs], a_r * h_i + a_i * h_r + buim[rows, cs])
                bure[rows, cs] = h_r
                buim[rows, cs] = h_i
            hre[:, cs] = h_r
            him[:, cs] = h_i
        ys.append(_bdot(bure[:, hs].astype(BF16), cre_ref[hh]) - _bdot(buim[:, hs].astype(BF16), cim_ref[hh]))
    y_tb = jnp.concatenate(ys, axis=1).reshape(t_len, bt, w)
    y_ref[...] = jnp.swapaxes(y_tb, 0, 1).astype(BF16)


def _s5_scan(u, mats, lc, reverse):
    bm, a_r, a_i, cre, cim = mats
    b, s, w = u.shape
    bt, t_len = S5_BATCH_TILE, S5_CHUNK
    nch, nctx = s // t_len, lc // t_len
    ns = a_r.shape[1]

    def chunk(k):
        if not reverse:
            return k
        return jnp.where(k < nctx, nctx - 1 - k, nch - 1 - (k - nctx))

    blk = pl.BlockSpec((bt, t_len, w), lambda g, k: (g, chunk(k), 0))
    return pl.pallas_call(
        functools.partial(_s5_scan_kernel, reverse=reverse),
        grid=(b // bt, nch),
        in_specs=[blk, _full_spec(bm.shape), _full_spec(a_r.shape), _full_spec(a_i.shape),
                  _full_spec(cre.shape), _full_spec(cim.shape)],
        out_specs=blk,
        out_shape=jax.ShapeDtypeStruct((b, s, w), BF16),
        scratch_shapes=[pltpu.VMEM((t_len * bt, ns), F32), pltpu.VMEM((t_len * bt, ns), F32),
                        pltpu.VMEM((bt, ns), F32), pltpu.VMEM((bt, ns), F32)],
        compiler_params=_cparams("parallel", "arbitrary"),
        name="s5_bwd" if reverse else "s5_fwd",
    )(u, bm, a_r, a_i, cre, cim)


def _hyb_mid_kernel(x_ref, m_ref, u_ref, yf_ref, yb_ref, zg_ref, d_ref, gw_ref, gb_ref, lg_ref, lb_ref,
                    ws_ref, bsx_ref, ow_ref, o_ref, *, nctx):
    seg = (pl.program_id(1) >= nctx).astype(jnp.int32)
    tm = x_ref.shape[1]
    w = u_ref.shape[2]
    y = u_ref[0].astype(F32) * d_ref[...] + yf_ref[0].astype(F32) + yb_ref[0].astype(F32)
    sg = _gelu_tanh(y)
    s5o = sg * _sigmoid(_bdot(sg.astype(BF16), gw_ref[...]) + gb_ref[...])
    zg = _gelu_tanh(zg_ref[0].astype(F32))
    gu = zg[:, :w]
    v = zg[:, w:]
    mu = jnp.mean(v, axis=-1, keepdims=True)
    vc = v - mu
    var = jnp.mean(vc * vc, axis=-1, keepdims=True)
    vn = (vc * lax.rsqrt(var + NORM_EPS) * lg_ref[...] + lb_ref[...]).astype(BF16)
    lane = lax.broadcasted_iota(jnp.int32, (GM_CHUNK, 128), 1)
    low = lane < (w // GM_HEADS)
    chunks = []
    for c in range(tm // GM_CHUNK):
        vch = vn[c * GM_CHUNK:(c + 1) * GM_CHUNK]
        cols = []
        for kp in range(GM_HEADS // 2):
            vs = vch[:, kp * 128:(kp + 1) * 128]
            cols.append(jnp.where(low, _bdot(ws_ref[2 * kp], vs), _bdot(ws_ref[2 * kp + 1], vs)))
        chunks.append(jnp.concatenate(cols, axis=1) + bsx_ref[...])
    gm = gu * jnp.concatenate(chunks, axis=0)
    ycat = jnp.concatenate([s5o, gm], axis=1).astype(BF16)
    o_ref[0] = x_ref[0] + m_ref[0, seg, 2:3, :] * _bdot(ycat, ow_ref[...])


def _hyb_mid(x, modsel, u, yf, yb, zg, d, glu_w, glu_b, ln_g, ln_b, ws, bsx, out_w, lc):
    b, s, dm = x.shape
    w = u.shape[2]
    tm = ROW_TILE
    vec = _full_spec((1, w))
    return pl.pallas_call(
        functools.partial(_hyb_mid_kernel, nctx=lc // tm),
        grid=(b, s // tm),
        in_specs=[_row_spec(tm, dm), _mod_spec(dm), _row_spec(tm, w), _row_spec(tm, w), _row_spec(tm, w),
                  _row_spec(tm, 2 * w), vec, _full_spec((w, w)), vec, vec, vec,
                  _full_spec(ws.shape), _full_spec(bsx.shape), _full_spec(out_w.shape)],
        out_specs=_row_spec(tm, dm),
        out_shape=jax.ShapeDtypeStruct((b, s, dm), F32),
        compiler_params=_cparams("parallel", "parallel"),
        name="hyb_mid",
    )(x, modsel, u, yf, yb, zg, d.reshape(1, w), glu_w, glu_b.reshape(1, w), ln_g.reshape(1, w),
      ln_b.reshape(1, w), ws, bsx, out_w)


def _mla_in_kernel(x_ref, m_ref, g_ref, win_ref, qng_ref, kvng_ref, wq_ref, wkv_ref, qn_ref, qr_ref, kn_ref,
                   kr_ref, rot_ref, cos_ref, sin_ref, q_out, k_out, v_out, *, nctx, q_lora, kv_lora, heads, scale):
    seg = (pl.program_id(1) >= nctx).astype(jnp.int32)
    h = _modulate(x_ref[0], g_ref[...], m_ref[0, seg, 0:1, :], m_ref[0, seg, 1:2, :])
    z = _bdot(h.astype(BF16), win_ref[...])
    cq = z[:, :q_lora]
    ckv = z[:, q_lora:q_lora + kv_lora]
    kr = z[:, q_lora + kv_lora:]
    q = _bdot((_rms(cq) * qng_ref[...]).astype(BF16), wq_ref[...])
    kv = _bdot((_rms(ckv) * kvng_ref[...]).astype(BF16), wkv_ref[...])
    cos = cos_ref[...]
    sin = sin_ref[...]

    def rope(t):
        return t * cos + _bdot(t.astype(BF16), rot_ref[...]) * sin

    k_rope = rope(_rms(kr, MLA_ROPE) * kr_ref[...]).astype(BF16)
    for hd in range(heads):
        qh = q[:, hd * MLA_QK:(hd + 1) * MLA_QK]
        q_out[0, hd, :, :MLA_NOPE] = (_rms(qh[:, :MLA_NOPE]) * qn_ref[...] * scale).astype(BF16)
        q_out[0, hd, :, MLA_NOPE:] = (rope(_rms(qh[:, MLA_NOPE:], MLA_ROPE) * qr_ref[...]) * scale).astype(BF16)
        kvh = kv[:, hd * (MLA_NOPE + MLA_V):(hd + 1) * (MLA_NOPE + MLA_V)]
        k_out[0, hd, :, :MLA_NOPE] = (_rms(kvh[:, :MLA_NOPE]) * kn_ref[...]).astype(BF16)
        k_out[0, hd, :, MLA_NOPE:] = k_rope
        v_out[0, hd] = kvh[:, MLA_NOPE:].astype(BF16)


def _mla_in(x, modsel, g, win, qng, kvng, wq, wkv, qn_g, qr_g, kn_g, kr_g, rot, cos, sin, lc, heads):
    b, s, d = x.shape
    tm = ROW_TILE
    q_lora, kv_lora = qng.shape[0], kvng.shape[0]
    scale = (MLA_NOPE + MLA_ROPE) ** -0.5 * math.log2(math.e)
    pad = jnp.zeros((MLA_QK - MLA_NOPE - MLA_ROPE,), F32)
    vec = lambda a: _full_spec((1, a.shape[-1]))
    args = [g.reshape(1, d), win, qng.reshape(1, -1), kvng.reshape(1, -1), wq, wkv, qn_g.reshape(1, -1),
            jnp.concatenate([qr_g, pad]).reshape(1, -1), kn_g.reshape(1, -1),
            jnp.concatenate([kr_g, pad]).reshape(1, -1), rot]
    tab = pl.BlockSpec((tm, 128), lambda bb, i: (i, 0))
    head_spec = lambda n: pl.BlockSpec((1, heads, tm, n), lambda bb, i: (bb, 0, i, 0))
    return pl.pallas_call(
        functools.partial(_mla_in_kernel, nctx=lc // tm, q_lora=q_lora, kv_lora=kv_lora, heads=heads, scale=scale),
        grid=(b, s // tm),
        in_specs=[_row_spec(tm, d), _mod_spec(d)] + [_full_spec(a.shape) for a in args] + [tab, tab],
        out_specs=[head_spec(MLA_QK), head_spec(MLA_QK), head_spec(MLA_V)],
        out_shape=[jax.ShapeDtypeStruct((b, heads, s, MLA_QK), BF16), jax.ShapeDtypeStruct((b, heads, s, MLA_QK), BF16),
                   jax.ShapeDtypeStruct((b, heads, s, MLA_V), BF16)],
        compiler_params=_cparams("parallel", "parallel"),
        name="mla_in",
    )(x, modsel, *args, cos, sin)


def _attn_kernel(q_ref, k_ref, v_ref, o_ref, *, lc, nq_ctx):
    i = pl.program_id(2)
    hp, s_len, dv = v_ref.shape[1], v_ref.shape[2], v_ref.shape[3]

    def run(n):
        for hd in range(hp):
            q = q_ref[0, hd]
            s = lax.dot_general(q, k_ref[0, hd, :n, :], (((1,), (1,)), ((), ())), preferred_element_type=F32)
            p = jnp.exp2(s - jnp.max(s, axis=-1, keepdims=True))
            l = jnp.sum(p, axis=-1, keepdims=True)
            o_ref[0, :, hd * dv:(hd + 1) * dv] = (_bdot(p.astype(BF16), v_ref[0, hd, :n, :]) / l).astype(BF16)

    pl.when(i < nq_ctx)(lambda: run(lc))
    pl.when(i >= nq_ctx)(lambda: run(s_len))


def _attention(q, k, v, lc):
    b, heads, s, dq = q.shape
    dv = v.shape[3]
    tq = ROW_TILE
    hp = ATTN_HEADS_PER_STEP
    return pl.pallas_call(
        functools.partial(_attn_kernel, lc=lc, nq_ctx=lc // tq),
        grid=(b, heads // hp, s // tq),
        in_specs=[pl.BlockSpec((1, hp, tq, dq), lambda bb, h, i: (bb, h, i, 0)),
                  pl.BlockSpec((1, hp, s, dq), lambda bb, h, i: (bb, h, 0, 0)),
                  pl.BlockSpec((1, hp, s, dv), lambda bb, h, i: (bb, h, 0, 0))],
        out_specs=pl.BlockSpec((1, tq, hp * dv), lambda bb, h, i: (bb, i, h)),
        out_shape=jax.ShapeDtypeStruct((b, s, heads * dv), BF16),
        compiler_params=_cparams("parallel", "parallel", "parallel"),
        name="mla_attn",
    )(q, k, v)


def _mla_out_kernel(x_ref, m_ref, o_ref, w_ref, out_ref, *, nctx):
    seg = (pl.program_id(1) >= nctx).astype(jnp.int32)
    out_ref[0] = x_ref[0] + m_ref[0, seg, 2:3, :] * _bdot(o_ref[0], w_ref[...])


def _mla_out(x, modsel, o, w, lc):
    b, s, d = x.shape
    tm = ROW_TILE
    return pl.pallas_call(
        functools.partial(_mla_out_kernel, nctx=lc // tm),
        grid=(b, s // tm),
        in_specs=[_row_spec(tm, d), _mod_spec(d), _row_spec(tm, o.shape[2]), _full_spec(w.shape)],
        out_specs=_row_spec(tm, d),
        out_shape=jax.ShapeDtypeStruct((b, s, d), F32),
        compiler_params=_cparams("parallel", "parallel"),
        name="mla_out",
    )(x, modsel, o, w)


def _router_kernel(x_ref, m_ref, g_ref, rwt_ref, rb_ref, hx_ref, meta_ref, counts_ref, run_ref, *, lc):
    tm, dm = x_ref.shape[1], x_ref.shape[2]

    @pl.when((pl.program_id(0) == 0) & (pl.program_id(1) == 0))
    def _():
        run_ref[...] = jnp.zeros_like(run_ref)

    is_ctx = pl.program_id(1) * tm + lax.broadcasted_iota(jnp.int32, (tm, 1), 0) < lc
    shift = jnp.where(is_ctx, m_ref[0, 0, 3:4, :], m_ref[0, 1, 3:4, :])
    scale = jnp.where(is_ctx, m_ref[0, 0, 4:5, :], m_ref[0, 1, 4:5, :])
    h = _modulate(x_ref[0], g_ref[...], shift, scale)
    hx_ref[0, :, :dm] = h
    logits = lax.dot_general(rwt_ref[...], h, (((1,), (1,)), ((), ())), precision=HIGHEST,
                             preferred_element_type=F32)
    scores = _sigmoid(logits)
    sel = scores + rb_ref[...]
    sel_r = [sel[e:e + 1, :] for e in range(N_EXPERTS)]
    sc_r = [scores[e:e + 1, :] for e in range(N_EXPERTS)]

    grp = []
    for g in range(N_GROUPS):
        a, b, c, d = sel_r[GROUP_SIZE * g:GROUP_SIZE * (g + 1)]
        hi1, lo1, hi2, lo2 = jnp.maximum(a, b), jnp.minimum(a, b), jnp.maximum(c, d), jnp.minimum(c, d)
        grp.append(jnp.maximum(hi1, hi2) + jnp.maximum(jnp.minimum(hi1, hi2), jnp.maximum(lo1, lo2)))
    best = jnp.zeros((1, tm), jnp.int32)
    cur = grp[0]
    for g in range(1, N_GROUPS):
        upd = grp[g] > cur
        best = jnp.where(upd, g, best)
        cur = jnp.where(upd, grp[g], cur)

    def in_group(rows, j):
        out = rows[j]
        for g in range(1, N_GROUPS):
            out = jnp.where(best == g, rows[GROUP_SIZE * g + j], out)
        return out

    v = [in_group(sel_r, j) for j in range(GROUP_SIZE)]
    sc = [in_group(sc_r, j) for j in range(GROUP_SIZE)]

    def first_argmax(vals):
        idx = jnp.zeros((1, tm), jnp.int32)
        top = vals[0]
        for j in range(1, GROUP_SIZE):
            upd = vals[j] > top
            idx = jnp.where(upd, j, idx)
            top = jnp.where(upd, vals[j], top)
        return idx

    i1 = first_argmax(v)
    i2 = first_argmax([jnp.where(i1 == j, -jnp.inf, v[j]) for j in range(GROUP_SIZE)])
    w1 = sum(jnp.where(i1 == j, sc[j], 0.0) for j in range(GROUP_SIZE))
    w2 = sum(jnp.where(i2 == j, sc[j], 0.0) for j in range(GROUP_SIZE))
    den = w1 + w2
    first_lo = i1 < i2
    lo = jnp.minimum(i1, i2)
    hi = jnp.maximum(i1, i2)
    w_lo = jnp.where(first_lo, w1, w2) / den
    w_hi = jnp.where(first_lo, w2, w1) / den
    pair = jnp.where(lo == 0, hi - 1, jnp.where(lo == 1, hi + 1, 5))
    cls = best * 6 + pair

    crow = lax.broadcasted_iota(jnp.int32, (CLASS_ROWS, tm), 0)
    onehot = crow == cls
    tri = (lax.broadcasted_iota(jnp.int32, (tm, tm), 0) <= lax.broadcasted_iota(jnp.int32, (tm, tm), 1))
    prefix = _bdot(jnp.where(onehot, 1.0, 0.0).astype(BF16), jnp.where(tri, 1.0, 0.0).astype(BF16))
    run = run_ref[...]
    rank = jnp.sum(jnp.where(onehot, prefix - 1.0 + run[:, 0:1], 0.0), axis=0, keepdims=True)
    run = run + prefix[:, tm - 1:tm]
    run_ref[...] = run
    counts_ref[...] = run

    r8 = lax.broadcasted_iota(jnp.int32, (8, tm), 0)
    meta_ref[...] = jnp.where(r8 == 0, cls.astype(F32), jnp.where(r8 == 1, rank, 0.0))
    rl = lax.broadcasted_iota(jnp.int32, (LANES, tm), 0)
    wt = jnp.where(rl == 0, w_lo, jnp.where(rl == 1, w_hi, 0.0))
    hx_ref[0, :, dm:] = wt.T


def _router(x, modsel, g, router_w, router_bias, lc):
    b, s, d = x.shape
    tm = ROUTER_TILE if s % ROUTER_TILE == 0 else ROW_TILE
    nt = s // tm
    return pl.pallas_call(
        functools.partial(_router_kernel, lc=lc),
        grid=(b, nt),
        in_specs=[_row_spec(tm, d), _mod_spec(d), _full_spec((1, d)), _full_spec((N_EXPERTS, d)),
                  _full_spec((N_EXPERTS, 1))],
        out_specs=[_row_spec(tm, d + LANES), pl.BlockSpec((8, tm), lambda bb, i: (0, bb * nt + i)),
                   _full_spec((CLASS_ROWS, LANES))],
        out_shape=[jax.ShapeDtypeStruct((b, s, d + LANES), F32), jax.ShapeDtypeStruct((8, b * s), F32),
                   jax.ShapeDtypeStruct((CLASS_ROWS, LANES), F32)],
        scratch_shapes=[pltpu.VMEM((CLASS_ROWS, LANES), F32)],
        compiler_params=_cparams("arbitrary", "arbitrary"),
        name="moe_router",
    )(x, modsel, g.reshape(1, d), router_w.T, router_bias.reshape(N_EXPERTS, 1))


def _row_copy(src, dst, sem):
    return pltpu.make_async_copy(src, dst, sem)


def _zero_fill(zpos_ref, zlen_ref, zbuf, xs_ref, sem, wait):
    for c in range(zpos_ref.shape[0]):
        units = zlen_ref[c]
        done = 0
        bit = zbuf.shape[0] // SUBLANES
        while bit >= 1:
            take = units & bit

            @pl.when(take != 0)
            def _(rows=bit * SUBLANES, done=done):
                start = zpos_ref[c] + done * SUBLANES
                cp = _row_copy(zbuf.at[pl.ds(0, rows)], xs_ref.at[pl.ds(start, rows)], sem)
                cp.wait() if wait else cp.start()

            done = done + take
            bit //= 2


def _dispatch_kernel(zpos_ref, zlen_ref, pos_ref, hx_ref, xs_ref, stage, zbuf, sem):
    tm = hx_ref.shape[0]

    @pl.when(pl.program_id(0) == 0)
    def _():
        zbuf[...] = jnp.zeros_like(zbuf)
        _zero_fill(zpos_ref, zlen_ref, zbuf, xs_ref, sem, False)
        _zero_fill(zpos_ref, zlen_ref, zbuf, xs_ref, sem, True)

    stage[...] = hx_ref[...].reshape(stage.shape)

    def start(i, carry):
        for k in range(SUBLANES):
            r = i * SUBLANES + k
            _row_copy(stage.at[r], xs_ref.at[pos_ref[0, 0, r]], sem).start(priority=k % 2)
        return carry

    lax.fori_loop(0, tm // SUBLANES, start, 0)
    _row_copy(stage, xs_ref.at[pl.ds(0, tm)], sem).wait()


def _dispatch(hx, pos, zpos, zlen, n_rows):
    n, dx = hx.shape
    tm = DISPATCH_TILE
    slab = (dx // LANES, LANES)
    return pl.pallas_call(
        _dispatch_kernel,
        grid_spec=pltpu.PrefetchScalarGridSpec(
            num_scalar_prefetch=2,
            grid=(n // tm,),
            in_specs=[pl.BlockSpec((1, 1, tm), lambda i, zp, zl: (i, 0, 0), memory_space=pltpu.SMEM),
                      pl.BlockSpec((tm, dx), lambda i, zp, zl: (i, 0))],
            out_specs=pl.BlockSpec(memory_space=pl.ANY),
            scratch_shapes=[pltpu.VMEM((tm,) + slab, F32), pltpu.VMEM((EXP_TILE,) + slab, F32),
                            pltpu.SemaphoreType.DMA(())],
        ),
        out_shape=jax.ShapeDtypeStruct((n_rows,) + slab, F32),
        compiler_params=_cparams("arbitrary"),
        name="moe_dispatch",
    )(zpos, zlen, pos.reshape(n // tm, 1, tm), hx)


def _experts_kernel(tblk_ref, elo_ref, ehi_ref, valid_ref, xs_ref, w1l, w3l, w2l, w1h, w3h, w2h, ys_ref):
    j = pl.program_id(0)
    tm = xs_ref.shape[0]
    d = ys_ref.shape[1] * ys_ref.shape[2]

    @pl.when(valid_ref[j] == 1)
    def _():
        xw = xs_ref[...].reshape(tm, xs_ref.shape[1] * xs_ref.shape[2])
        x = xw[:, :d].astype(BF16)

        def hidden(w1, w3, w):
            a = _bdot(x, w1[0])
            return ((a * _sigmoid(a)) * _bdot(x, w3[0]) * w).astype(BF16)

        y = (_bdot(hidden(w1l, w3l, xw[:, d:d + 1]), w2l[0]) + _bdot(hidden(w1h, w3h, xw[:, d + 1:d + 2]), w2h[0]))
        ys_ref[...] = y.reshape(ys_ref.shape)

    @pl.when(valid_ref[j] == 0)
    def _():
        ys_ref[...] = jnp.zeros_like(ys_ref)


def _experts(xs, tblk, elo, ehi, valid, w1, w3, w2, n_tiles):
    ne, d, f = w1.shape
    tm = EXP_TILE
    wlo = lambda shape: pl.BlockSpec(shape, lambda j, tb, lo, hi, va: (lo[j], 0, 0))
    whi = lambda shape: pl.BlockSpec(shape, lambda j, tb, lo, hi, va: (hi[j], 0, 0))
    return pl.pallas_call(
        _experts_kernel,
        grid_spec=pltpu.PrefetchScalarGridSpec(
            num_scalar_prefetch=4,
            grid=(n_tiles,),
            in_specs=[pl.BlockSpec((tm,) + xs.shape[1:], lambda j, tb, lo, hi, va: (tb[j], 0, 0)),
                      wlo((1, d, f)), wlo((1, d, f)), wlo((1, f, d)), whi((1, d, f)), whi((1, d, f)), whi((1, f, d))],
            out_specs=pl.BlockSpec((tm, d // LANES, LANES), lambda j, tb, lo, hi, va: (j, 0, 0)),
        ),
        out_shape=jax.ShapeDtypeStruct((n_tiles * tm, d // LANES, LANES), F32),
        compiler_params=_cparams("arbitrary"),
        name="moe_experts",
    )(tblk, elo, ehi, valid, xs, w1, w3, w2, w1, w3, w2)


def _combine_kernel(pos_ref, posn_ref, x_ref, m_ref, ys_ref, o_ref, buf, sem, *, nt, nctx, skip):
    t = pl.program_id(0)
    n = pl.num_programs(0)
    tm = x_ref.shape[1]
    slot = t % 2

    def gather(p_ref, sl):
        def body(i, carry):
            for k in range(SUBLANES):
                r = i * SUBLANES + k
                _row_copy(ys_ref.at[p_ref[0, 0, r]], buf.at[sl, r], sem.at[sl]).start(priority=k % 2)
            return carry
        lax.fori_loop(0, tm // SUBLANES, body, 0)

    @pl.when(t == 0)
    def _():
        gather(pos_ref, 0)

    @pl.when(t + 1 < n)
    def _():
        gather(posn_ref, 1 - slot)

    _row_copy(ys_ref.at[pl.ds(0, tm)], buf.at[slot], sem.at[slot]).wait()
    seg = ((t % nt) + skip >= nctx).astype(jnp.int32)
    o_ref[0] = x_ref[0] + m_ref[0, seg, 5:6, :] * buf[slot].reshape(tm, x_ref.shape[2])


def _combine(x, modsel, ys, pos, lc, latent_only):
    b, s, d = x.shape
    tm = ROW_TILE
    nctx = lc // tm
    skip = nctx if latent_only else 0
    nt = s // tm - skip
    n = b * nt
    pos3 = pos.reshape(b, s // tm, tm)[:, skip:].reshape(n, 1, tm)
    return pl.pallas_call(
        functools.partial(_combine_kernel, nt=nt, nctx=nctx, skip=skip),
        grid=(n,),
        in_specs=[pl.BlockSpec((1, 1, tm), lambda t: (t, 0, 0), memory_space=pltpu.SMEM),
                  pl.BlockSpec((1, 1, tm), lambda t: (jnp.minimum(t + 1, n - 1), 0, 0), memory_space=pltpu.SMEM),
                  pl.BlockSpec((1, tm, d), lambda t: (t // nt, t % nt + skip, 0)),
                  pl.BlockSpec((1, 2, 6, d), lambda t: (t // nt, 0, 0, 0)),
                  pl.BlockSpec(memory_space=pl.ANY)],
        out_specs=pl.BlockSpec((1, tm, d), lambda t: (t // nt, t % nt, 0)),
        out_shape=jax.ShapeDtypeStruct((b, nt * tm, d), F32),
        scratch_shapes=[pltpu.VMEM((2, tm) + ys.shape[1:], F32), pltpu.SemaphoreType.DMA((2,))],
        compiler_params=_cparams("arbitrary"),
        name="moe_combine",
    )(pos3, pos3, x, modsel, ys)


def _class_experts():
    lo, hi = [], []
    for g in range(N_GROUPS):
        for a in range(GROUP_SIZE):
            for c in range(a + 1, GROUP_SIZE):
                lo.append(GROUP_SIZE * g + a)
                hi.append(GROUP_SIZE * g + c)
    return jnp.array(lo, jnp.int32), jnp.array(hi, jnp.int32)


def _moe(x, modsel, g, router_w, router_bias, w1, w3, w2, lc, latent_only):
    b, s, d = x.shape
    n = b * s
    tm = EXP_TILE
    hx, meta, counts = _router(x, modsel, g, router_w, router_bias, lc)

    cnt = counts[:N_CLASSES, 0].astype(jnp.int32)
    padded = (cnt + tm - 1) // tm * tm
    ends = jnp.cumsum(padded)
    off = ends - padded
    pos = off[meta[0].astype(jnp.int32)] + meta[1].astype(jnp.int32)
    n_tiles = n // tm + N_CLASSES
    start = jnp.arange(n_tiles, dtype=jnp.int32) * tm
    valid = start < ends[-1]
    last = ends[-1] // tm - 1
    tblk = jnp.minimum(jnp.arange(n_tiles, dtype=jnp.int32), last)
    tcls = jnp.minimum(jnp.sum(ends[None, :] <= (tblk * tm)[:, None], axis=1), N_CLASSES - 1).astype(jnp.int32)
    lo_tab, hi_tab = _class_experts()

    n_rows = n_tiles * tm
    pad0 = (off + cnt) // SUBLANES * SUBLANES
    tail = ends[-1] + jnp.arange(N_CLASSES, dtype=jnp.int32) * tm
    zpos = jnp.concatenate([pad0, jnp.minimum(tail, n_rows - tm)]).astype(jnp.int32)
    zlen = jnp.concatenate([(ends - pad0) // SUBLANES, jnp.where(tail < n_rows, tm // SUBLANES, 0)]).astype(jnp.int32)
    xs = _dispatch(hx.reshape(n, d + LANES), pos, zpos, zlen, n_rows)
    ys = _experts(xs, tblk, lo_tab[tcls], hi_tab[tcls], valid.astype(jnp.int32), w1, w3, w2, n_tiles)
    return _combine(x, modsel, ys, pos, lc, latent_only)


def _rope_tables(lc, l):
    half = MLA_ROPE // 4
    freqs = ROPE_BASE ** (-jnp.arange(half, dtype=F32) / half)
    pos = jnp.arange(l)
    pos_row = (pos // GRID_W).astype(F32)
    pos_col = (pos % GRID_W).astype(F32)
    ang = jnp.concatenate([pos_row[:, None] * freqs, pos_row[:, None] * freqs,
                           pos_col[:, None] * freqs, pos_col[:, None] * freqs], axis=1)
    pad = jnp.zeros((l, 128 - MLA_ROPE), F32)
    cos = jnp.concatenate([jnp.concatenate([jnp.ones((lc, MLA_ROPE), F32), jnp.zeros((lc, 128 - MLA_ROPE), F32)], axis=1),
                           jnp.concatenate([jnp.cos(ang), pad], axis=1)], axis=0)
    sin = jnp.concatenate([jnp.zeros((lc, 128), F32), jnp.concatenate([jnp.sin(ang), pad], axis=1)], axis=0)
    r = jnp.zeros((128, 128), F32)
    idx = jnp.arange(half)
    for base in (0, MLA_ROPE // 2):
        r = r.at[base + half + idx, base + idx].set(-1.0)
        r = r.at[base + idx, base + half + idx].set(1.0)
    return cos, sin, r.astype(BF16)


def _pad_heads(w, heads):
    k = w.shape[0]
    w = w.reshape(k, heads, MLA_NOPE + MLA_ROPE)
    w = jnp.concatenate([w, jnp.zeros((k, heads, MLA_QK - MLA_NOPE - MLA_ROPE), w.dtype)], axis=2)
    return w.reshape(k, heads * MLA_QK)


def kernel(x, c, ctx, c_ctx, mod_w, mod_b, norm1_g, norm2_g, hyb_in_w, hyb_out_w, s5_a_re, s5_a_im, s5_log_dt, s5_b_re, s5_b_im, s5_c_re, s5_c_im, s5_d, s5_glu_w, s5_glu_b, gm_ln_g, gm_ln_b, gm_ws, gm_bs, mla_in_w, mla_q_norm_g, mla_kv_norm_g, mla_wq_b, mla_wkv_b, mla_qn_g, mla_qr_g, mla_kn_g, mla_kr_g, mla_out_w, router_w, router_bias, moe_w1, moe_w3, moe_w2):
    b, l, d = x.shape
    lc = ctx.shape[1]
    depth = mod_w.shape[0]
    heads = mla_out_w.shape[1] // MLA_V
    assert lc % ROW_TILE == 0 and l % ROW_TILE == 0 and b % S5_BATCH_TILE == 0

    xs = jnp.concatenate([ctx, x], axis=1)

    bp = -(-(b + 1) // 8) * 8
    cond = jnp.concatenate([c, c_ctx[None], jnp.zeros((bp - b - 1, d), F32)], axis=0)
    mods = _adaln(cond, mod_w, mod_b)

    cos, sin, rot = _rope_tables(lc, l)

    for layer in range(depth):
        i = layer // 2
        m = mods[layer].reshape(bp, 6, d)
        modsel = jnp.stack([jnp.broadcast_to(m[b][None], (b, 6, d)), m[:b]], axis=1)

        if layer % 2 == 0:
            u, zg = _hyb_in(xs, modsel, norm1_g[layer], hyb_in_w[i].astype(BF16), lc)
            ab_re, ab_im, bb_re, bb_im = _s5_discretize(s5_a_re[i], s5_a_im[i], s5_log_dt[i], s5_b_re[i], s5_b_im[i])
            ys = []
            for dr in range(2):
                mats = _s5_matrices(ab_re[dr], ab_im[dr], bb_re[dr], bb_im[dr], s5_c_re[i, dr], s5_c_im[i, dr])
                ys.append(_s5_scan(u, mats, lc, reverse=bool(dr)))
            hd = (d // 2) // GM_HEADS
            bsx = jnp.repeat(gm_bs[i].T, hd, axis=1)
            xs = _hyb_mid(xs, modsel, u, ys[0], ys[1], zg, s5_d[i], s5_glu_w[i].astype(BF16), s5_glu_b[i],
                          gm_ln_g[i], gm_ln_b[i], gm_ws[i].astype(BF16), bsx, hyb_out_w[i].astype(BF16), lc)
        else:
            q_lora = mla_q_norm_g.shape[1]
            kv_lora = mla_kv_norm_g.shape[1]
            win = jnp.concatenate([mla_in_w[i], jnp.zeros((d, 128 - MLA_ROPE), F32)], axis=1).astype(BF16)
            wq = _pad_heads(mla_wq_b[i], heads).astype(BF16)
            q, k, v = _mla_in(xs, modsel, norm1_g[layer], win, mla_q_norm_g[i], mla_kv_norm_g[i], wq,
                              mla_wkv_b[i].astype(BF16), mla_qn_g[i], mla_qr_g[i], mla_kn_g[i], mla_kr_g[i],
                              rot, cos, sin, lc, heads)
            o = _attention(q, k, v, lc)
            xs = _mla_out(xs, modsel, o, mla_out_w[i].astype(BF16), lc)

        xs = _moe(xs, modsel, norm2_g[layer], router_w, router_bias, moe_w1[layer].astype(BF16),
                  moe_w3[layer].astype(BF16), moe_w2[layer].astype(BF16), lc, latent_only=layer == depth - 1)

    return xs
```

```python
import functools
import math

import jax
import jax.numpy as jnp
from jax import lax
from jax.experimental import pallas as pl
from jax.experimental.pallas import tpu as pltpu

F32 = jnp.float32
BF16 = jnp.bfloat16
HIGHEST = lax.Precision.HIGHEST

NORM_EPS = 1e-6
ROPE_BASE = 10000.0
GRID_W = 64

S5_GROUP_CH = 16
S5_STATE = 64
S5_BATCH_TILE = 8
S5_CHUNK = 128
S5_COLS = 512

GM_HEADS = 8
GM_CHUNK = 128

MLA_NOPE = 128
MLA_ROPE = 64
MLA_V = 128
MLA_QK = 256

N_EXPERTS = 16
N_GROUPS = 4
GROUP_SIZE = 4

ROW_TILE = 256
ATTN_HEADS_PER_STEP = 8
ROUTER_TILE = 768
N_CLASSES = 24
CLASS_ROWS = 32
EXP_TILE = 512
DISPATCH_TILE = 512
LANES = 128
SUBLANES = 8
VMEM_LIMIT = 56 * 1024 * 1024


def _cparams(*sem):
    return pltpu.CompilerParams(dimension_semantics=sem, vmem_limit_bytes=VMEM_LIMIT)


def _sigmoid(x):
    return 1.0 / (1.0 + jnp.exp(-x))


def _gelu_tanh(x):
    c = math.sqrt(2.0 / math.pi)
    return 0.5 * x * (1.0 + jnp.tanh(c * (x + 0.044715 * (x * x * x))))


def _rms(x, n=None):
    n = x.shape[-1] if n is None else n
    return x * lax.rsqrt(jnp.sum(x * x, axis=-1, keepdims=True) * (1.0 / n) + NORM_EPS)


def _modulate(x, g, shift, scale):
    return (_rms(x) * g) * (1.0 + scale) + shift


def _bdot(a, b):
    return jnp.dot(a, b, preferred_element_type=F32)


def _adaln_kernel(c_ref, w_ref, b_ref, o_ref):
    c = c_ref[...]
    s = c * _sigmoid(c)
    o_ref[0] = jnp.dot(s, w_ref[0], precision=HIGHEST, preferred_element_type=F32) + b_ref[0]


def _adaln(cond, mod_w, mod_b):
    depth, d, n = mod_w.shape
    bp = cond.shape[0]
    tn = n // 4
    return pl.pallas_call(
        _adaln_kernel,
        grid=(depth, n // tn),
        in_specs=[
            pl.BlockSpec((bp, d), lambda l, j: (0, 0)),
            pl.BlockSpec((1, d, tn), lambda l, j: (l, 0, j)),
            pl.BlockSpec((1, 1, tn), lambda l, j: (l, 0, j)),
        ],
        out_specs=pl.BlockSpec((1, bp, tn), lambda l, j: (l, 0, j)),
        out_shape=jax.ShapeDtypeStruct((depth, bp, n), F32),
        compiler_params=_cparams("parallel", "parallel"),
        name="adaln",
    )(cond, mod_w, mod_b.reshape(depth, 1, n))


def _mod_spec(d):
    return pl.BlockSpec((1, 2, 6, d), lambda b, i: (b, 0, 0, 0))


def _row_spec(tm, n):
    return pl.BlockSpec((1, tm, n), lambda b, i: (b, i, 0))


def _full_spec(shape):
    nd = len(shape)
    return pl.BlockSpec(shape, lambda b, i: (0,) * nd)


def _seg_vec(m_ref, k, tm, lc):
    is_ctx = pl.program_id(1) * tm + lax.broadcasted_iota(jnp.int32, (tm, 1), 0) < lc
    return jnp.where(is_ctx, m_ref[0, 0, k:k + 1, :], m_ref[0, 1, k:k + 1, :])


def _hyb_in_kernel(x_ref, m_ref, g_ref, w_ref, u_ref, zg_ref, *, lc, s5w):
    tm = x_ref.shape[1]
    h = _modulate(x_ref[0], g_ref[...], _seg_vec(m_ref, 0, tm, lc), _seg_vec(m_ref, 1, tm, lc))
    z = _bdot(h.astype(BF16), w_ref[...])
    u_ref[0] = z[:, :s5w].astype(BF16)
    zg_ref[0] = z[:, s5w:].astype(BF16)


def _hyb_in(x, modsel, g, w, lc):
    b, s, d = x.shape
    n = w.shape[1]
    s5w = d // 2
    tm = ROUTER_TILE if s % ROUTER_TILE == 0 else ROW_TILE
    return pl.pallas_call(
        functools.partial(_hyb_in_kernel, lc=lc, s5w=s5w),
        grid=(b, s // tm),
        in_specs=[_row_spec(tm, d), _mod_spec(d), _full_spec((1, d)), _full_spec((d, n))],
        out_specs=[_row_spec(tm, s5w), _row_spec(tm, n - s5w)],
        out_shape=[jax.ShapeDtypeStruct((b, s, s5w), BF16), jax.ShapeDtypeStruct((b, s, n - s5w), BF16)],
        compiler_params=_cparams("parallel", "parallel"),
        name="hyb_in",
    )(x, modsel, g.reshape(1, d), w)


def _s5_disc_kernel(are_ref, aim_ref, ldt_ref, bre_ref, bim_ref, abre_ref, abim_ref, bbre_ref, bbim_ref):
    a_re = are_ref[0]
    a_im = aim_ref[0]
    dt = jnp.exp(ldt_ref[0])
    mag = jnp.exp(dt * a_re)
    ab_re = mag * jnp.cos(dt * a_im)
    ab_im = mag * jnp.sin(dt * a_im)
    nr = ab_re - 1.0
    ni = ab_im
    den = a_re * a_re + a_im * a_im
    f_re = (nr * a_re + ni * a_im) / den
    f_im = (ni * a_re - nr * a_im) / den
    abre_ref[0] = ab_re
    abim_ref[0] = ab_im
    b_re = bre_ref[0]
    b_im = bim_ref[0]
    bbre_ref[0] = f_re[None] * b_re - f_im[None] * b_im
    bbim_ref[0] = f_re[None] * b_im + f_im[None] * b_re


def _s5_discretize(a_re, a_im, log_dt, b_re, b_im):
    nd, g, n = a_re.shape
    p = b_re.shape[-1]
    bt_re = jnp.transpose(b_re, (0, 3, 1, 2))
    bt_im = jnp.transpose(b_im, (0, 3, 1, 2))
    s3 = pl.BlockSpec((1, g, n), lambda i: (i, 0, 0))
    s4 = pl.BlockSpec((1, p, g, n), lambda i: (i, 0, 0, 0))
    return pl.pallas_call(
        _s5_disc_kernel,
        grid=(nd,),
        in_specs=[s3, s3, pl.BlockSpec((1, g, 1), lambda i: (i, 0, 0)), s4, s4],
        out_specs=[s3, s3, s4, s4],
        out_shape=[jax.ShapeDtypeStruct((nd, g, n), F32)] * 2 + [jax.ShapeDtypeStruct((nd, p, g, n), F32)] * 2,
        compiler_params=_cparams("parallel"),
        name="s5_disc",
    )(a_re, a_im, log_dt.reshape(nd, g, 1), bt_re, bt_im)


def _s5_matrices(ab_re, ab_im, bb_re, bb_im, c_re, c_im):
    p, g, n = bb_re.shape
    gh = g // 2
    eye = jnp.eye(gh, dtype=F32)

    def in_half(bb, hh):
        blk = jnp.transpose(bb[:, hh * gh:(hh + 1) * gh], (1, 0, 2))
        return jnp.einsum("gpn,gk->gpkn", blk, eye).reshape(gh * p, gh * n)

    def out_half(c, hh):
        blk = c[hh * gh:(hh + 1) * gh]
        return jnp.einsum("gpn,gk->gnkp", blk, eye).reshape(gh * n, gh * p)

    bm = jnp.stack([jnp.concatenate([in_half(bb_re, hh), in_half(bb_im, hh)], axis=1) for hh in range(2)])
    cre = jnp.stack([out_half(c_re, hh) for hh in range(2)])
    cim = jnp.stack([out_half(c_im, hh) for hh in range(2)])
    return (bm.astype(BF16), ab_re.reshape(1, g * n), ab_im.reshape(1, g * n), cre.astype(BF16), cim.astype(BF16))


def _s5_scan_kernel(u_ref, bm_ref, are_ref, aim_ref, cre_ref, cim_ref, y_ref, bure, buim, hre, him, *, reverse):
    bt, t_len, w = u_ref.shape
    ns = hre.shape[1]
    half = ns // 2

    @pl.when(pl.program_id(1) == 0)
    def _():
        hre[...] = jnp.zeros_like(hre)
        him[...] = jnp.zeros_like(him)

    u_tb = jnp.swapaxes(u_ref[...].astype(F32), 0, 1).reshape(t_len * bt, w).astype(BF16)
    ys = []
    for hh in range(2):
        hs = slice(hh * half, (hh + 1) * half)
        bu = _bdot(u_tb[:, hh * (w // 2):(hh + 1) * (w // 2)], bm_ref[hh])
        bure[:, hs] = bu[:, :half]
        buim[:, hs] = bu[:, half:]
        for c0 in range(hh * half, (hh + 1) * half, S5_COLS):
            cs = slice(c0, c0 + S5_COLS)
            a_r = jnp.broadcast_to(are_ref[:, cs], (bt, S5_COLS))
            a_i = jnp.broadcast_to(aim_ref[:, cs], (bt, S5_COLS))
            h_r, h_i = hre[:, cs], him[:, cs]
            for t in range(t_len):
                tt = (t_len - 1 - t) if reverse else t
                rows = slice(tt * bt, (tt + 1) * bt)
                h_r, h_i = (a_r * h_r - a_i * h_i + bure[rows, cs], a_r * h_i + a_i * h_r + buim[rows, cs])
                bure[rows, cs] = h_r
                buim[rows, cs] = h_i
            hre[:, cs] = h_r
            him[:, cs] = h_i
        ys.append(_bdot(bure[:, hs].astype(BF16), cre_ref[hh]) - _bdot(buim[:, hs].astype(BF16), cim_ref[hh]))
    y_tb = jnp.concatenate(ys, axis=1).reshape(t_len, bt, w)
    y_ref[...] = jnp.swapaxes(y_tb, 0, 1).astype(BF16)


def _s5_scan(u, mats, lc, reverse):
    bm, a_r, a_i, cre, cim = mats
    b, s, w = u.shape
    bt, t_len = S5_BATCH_TILE, S5_CHUNK
    nch, nctx = s // t_len, lc // t_len
    ns = a_r.shape[1]

    def chunk(k):
        if not reverse:
            return k
        return jnp.where(k < nctx, nctx - 1 - k, nch - 1 - (k - nctx))

    blk = pl.BlockSpec((bt, t_len, w), lambda g, k: (g, chunk(k), 0))
    return pl.pallas_call(
        functools.partial(_s5_scan_kernel, reverse=reverse),
        grid=(b // bt, nch),
        in_specs=[blk, _full_spec(bm.shape), _full_spec(a_r.shape), _full_spec(a_i.shape),
                  _full_spec(cre.shape), _full_spec(cim.shape)],
        out_specs=blk,
        out_shape=jax.ShapeDtypeStruct((b, s, w), BF16),
        scratch_shapes=[pltpu.VMEM((t_len * bt, ns), F32), pltpu.VMEM((t_len * bt, ns), F32),
                        pltpu.VMEM((bt, ns), F32), pltpu.VMEM((bt, ns), F32)],
        compiler_params=_cparams("parallel", "arbitrary"),
        name="s5_bwd" if reverse else "s5_fwd",
    )(u, bm, a_r, a_i, cre, cim)


def _hyb_mid_kernel(x_ref, m_ref, u_ref, yf_ref, yb_ref, zg_ref, d_ref, gw_ref, gb_ref, lg_ref, lb_ref,
                    ws_ref, bsx_ref, ow_ref, o_ref, *, nctx):
    seg = (pl.program_id(1) >= nctx).astype(jnp.int32)
    tm = x_ref.shape[1]
    w = u_ref.shape[2]
    y = u_ref[0].astype(F32) * d_ref[...] + yf_ref[0].astype(F32) + yb_ref[0].astype(F32)
    sg = _gelu_tanh(y)
    s5o = sg * _sigmoid(_bdot(sg.astype(BF16), gw_ref[...]) + gb_ref[...])
    zg = _gelu_tanh(zg_ref[0].astype(F32))
    gu = zg[:, :w]
    v = zg[:, w:]
    mu = jnp.mean(v, axis=-1, keepdims=True)
    vc = v - mu
    var = jnp.mean(vc * vc, axis=-1, keepdims=True)
    vn = (vc * lax.rsqrt(var + NORM_EPS) * lg_ref[...] + lb_ref[...]).astype(BF16)
    lane = lax.broadcasted_iota(jnp.int32, (GM_CHUNK, 128), 1)
    low = lane < (w // GM_HEADS)
    chunks = []
    for c in range(tm // GM_CHUNK):
        vch = vn[c * GM_CHUNK:(c + 1) * GM_CHUNK]
        cols = []
        for kp in range(GM_HEADS // 2):
            vs = vch[:, kp * 128:(kp + 1) * 128]
            cols.append(jnp.where(low, _bdot(ws_ref[2 * kp], vs), _bdot(ws_ref[2 * kp + 1], vs)))
        chunks.append(jnp.concatenate(cols, axis=1) + bsx_ref[...])
    gm = gu * jnp.concatenate(chunks, axis=0)
    ycat = jnp.concatenate([s5o, gm], axis=1).astype(BF16)
    o_ref[0] = x_ref[0] + m_ref[0, seg, 2:3, :] * _bdot(ycat, ow_ref[...])


def _hyb_mid(x, modsel, u, yf, yb, zg, d, glu_w, glu_b, ln_g, ln_b, ws, bsx, out_w, lc):
    b, s, dm = x.shape
    w = u.shape[2]
    tm = ROW_TILE
    vec = _full_spec((1, w))
    return pl.pallas_call(
        functools.partial(_hyb_mid_kernel, nctx=lc // tm),
        grid=(b, s // tm),
        in_specs=[_row_spec(tm, dm), _mod_spec(dm), _row_spec(tm, w), _row_spec(tm, w), _row_spec(tm, w),
                  _row_spec(tm, 2 * w), vec, _full_spec((w, w)), vec, vec, vec,
                  _full_spec(ws.shape), _full_spec(bsx.shape), _full_spec(out_w.shape)],
        out_specs=_row_spec(tm, dm),
        out_shape=jax.ShapeDtypeStruct((b, s, dm), F32),
        compiler_params=_cparams("parallel", "parallel"),
        name="hyb_mid",
    )(x, modsel, u, yf, yb, zg, d.reshape(1, w), glu_w, glu_b.reshape(1, w), ln_g.reshape(1, w),
      ln_b.reshape(1, w), ws, bsx, out_w)


def _mla_in_kernel(x_ref, m_ref, g_ref, win_ref, qng_ref, kvng_ref, wq_ref, wkv_ref, qn_ref, qr_ref, kn_ref,
                   kr_ref, rot_ref, cos_ref, sin_ref, q_out, k_out, v_out, *, nctx, q_lora, kv_lora, heads, scale):
    seg = (pl.program_id(1) >= nctx).astype(jnp.int32)
    h = _modulate(x_ref[0], g_ref[...], m_ref[0, seg, 0:1, :], m_ref[0, seg, 1:2, :])
    z = _bdot(h.astype(BF16), win_ref[...])
    cq = z[:, :q_lora]
    ckv = z[:, q_lora:q_lora + kv_lora]
    kr = z[:, q_lora + kv_lora:]
    q = _bdot((_rms(cq) * qng_ref[...]).astype(BF16), wq_ref[...])
    kv = _bdot((_rms(ckv) * kvng_ref[...]).astype(BF16), wkv_ref[...])
    cos = cos_ref[...]
    sin = sin_ref[...]

    def rope(t):
        return t * cos + _bdot(t.astype(BF16), rot_ref[...]) * sin

    k_rope = rope(_rms(kr, MLA_ROPE) * kr_ref[...]).astype(BF16)
    for hd in range(heads):
        qh = q[:, hd * MLA_QK:(hd + 1) * MLA_QK]
        q_out[0, hd, :, :MLA_NOPE] = (_rms(qh[:, :MLA_NOPE]) * qn_ref[...] * scale).astype(BF16)
        q_out[0, hd, :, MLA_NOPE:] = (rope(_rms(qh[:, MLA_NOPE:], MLA_ROPE) * qr_ref[...]) * scale).astype(BF16)
        kvh = kv[:, hd * (MLA_NOPE + MLA_V):(hd + 1) * (MLA_NOPE + MLA_V)]
        k_out[0, hd, :, :MLA_NOPE] = (_rms(kvh[:, :MLA_NOPE]) * kn_ref[...]).astype(BF16)
        k_out[0, hd, :, MLA_NOPE:] = k_rope
        v_out[0, hd] = kvh[:, MLA_NOPE:].astype(BF16)


def _mla_in(x, modsel, g, win, qng, kvng, wq, wkv, qn_g, qr_g, kn_g, kr_g, rot, cos, sin, lc, heads):
    b, s, d = x.shape
    tm = ROW_TILE
    q_lora, kv_lora = qng.shape[0], kvng.shape[0]
    scale = (MLA_NOPE + MLA_ROPE) ** -0.5 * math.log2(math.e)
    pad = jnp.zeros((MLA_QK - MLA_NOPE - MLA_ROPE,), F32)
    vec = lambda a: _full_spec((1, a.shape[-1]))
    args = [g.reshape(1, d), win, qng.reshape(1, -1), kvng.reshape(1, -1), wq, wkv, qn_g.reshape(1, -1),
            jnp.concatenate([qr_g, pad]).reshape(1, -1), kn_g.reshape(1, -1),
            jnp.concatenate([kr_g, pad]).reshape(1, -1), rot]
    tab = pl.BlockSpec((tm, 128), lambda bb, i: (i, 0))
    head_spec = lambda n: pl.BlockSpec((1, heads, tm, n), lambda bb, i: (bb, 0, i, 0))
    return pl.pallas_call(
        functools.partial(_mla_in_kernel, nctx=lc // tm, q_lora=q_lora, kv_lora=kv_lora, heads=heads, scale=scale),
        grid=(b, s // tm),
        in_specs=[_row_spec(tm, d), _mod_spec(d)] + [_full_spec(a.shape) for a in args] + [tab, tab],
        out_specs=[head_spec(MLA_QK), head_spec(MLA_QK), head_spec(MLA_V)],
        out_shape=[jax.ShapeDtypeStruct((b, heads, s, MLA_QK), BF16), jax.ShapeDtypeStruct((b, heads, s, MLA_QK), BF16),
                   jax.ShapeDtypeStruct((b, heads, s, MLA_V), BF16)],
        compiler_params=_cparams("parallel", "parallel"),
        name="mla_in",
    )(x, modsel, *args, cos, sin)


def _attn_kernel(q_ref, k_ref, v_ref, o_ref, *, lc, nq_ctx):
    i = pl.program_id(2)
    hp, s_len, dv = v_ref.shape[1], v_ref.shape[2], v_ref.shape[3]

    def run(n):
        for hd in range(hp):
            q = q_ref[0, hd]
            s = lax.dot_general(q, k_ref[0, hd, :n, :], (((1,), (1,)), ((), ())), preferred_element_type=F32)
            p = jnp.exp2(s - jnp.max(s, axis=-1, keepdims=True))
            l = jnp.sum(p, axis=-1, keepdims=True)
            o_ref[0, :, hd * dv:(hd + 1) * dv] = (_bdot(p.astype(BF16), v_ref[0, hd, :n, :]) / l).astype(BF16)

    pl.when(i < nq_ctx)(lambda: run(lc))
    pl.when(i >= nq_ctx)(lambda: run(s_len))


def _attention(q, k, v, lc):
    b, heads, s, dq = q.shape
    dv = v.shape[3]
    tq = ROW_TILE
    hp = ATTN_HEADS_PER_STEP
    return pl.pallas_call(
        functools.partial(_attn_kernel, lc=lc, nq_ctx=lc // tq),
        grid=(b, heads // hp, s // tq),
        in_specs=[pl.BlockSpec((1, hp, tq, dq), lambda bb, h, i: (bb, h, i, 0)),
                  pl.BlockSpec((1, hp, s, dq), lambda bb, h, i: (bb, h, 0, 0)),
                  pl.BlockSpec((1, hp, s, dv), lambda bb, h, i: (bb, h, 0, 0))],
        out_specs=pl.BlockSpec((1, tq, hp * dv), lambda bb, h, i: (bb, i, h)),
        out_shape=jax.ShapeDtypeStruct((b, s, heads * dv), BF16),
        compiler_params=_cparams("parallel", "parallel", "parallel"),
        name="mla_attn",
    )(q, k, v)


def _mla_out_kernel(x_ref, m_ref, o_ref, w_ref, out_ref, *, lc):
    out_ref[0] = x_ref[0] + _seg_vec(m_ref, 2, x_ref.shape[1], lc) * _bdot(o_ref[0], w_ref[...])


def _mla_out(x, modsel, o, w, lc):
    b, s, d = x.shape
    tm = ROUTER_TILE if s % ROUTER_TILE == 0 else ROW_TILE
    return pl.pallas_call(
        functools.partial(_mla_out_kernel, lc=lc),
        grid=(b, s // tm),
        in_specs=[_row_spec(tm, d), _mod_spec(d), _row_spec(tm, o.shape[2]), _full_spec(w.shape)],
        out_specs=_row_spec(tm, d),
        out_shape=jax.ShapeDtypeStruct((b, s, d), F32),
        compiler_params=_cparams("parallel", "parallel"),
        name="mla_out",
    )(x, modsel, o, w)


def _router_kernel(x_ref, m_ref, g_ref, rwt_ref, rb_ref, hx_ref, meta_ref, counts_ref, run_ref, *, lc):
    tm, dm = x_ref.shape[1], x_ref.shape[2]

    @pl.when((pl.program_id(0) == 0) & (pl.program_id(1) == 0))
    def _():
        run_ref[...] = jnp.zeros_like(run_ref)

    is_ctx = pl.program_id(1) * tm + lax.broadcasted_iota(jnp.int32, (tm, 1), 0) < lc
    shift = jnp.where(is_ctx, m_ref[0, 0, 3:4, :], m_ref[0, 1, 3:4, :])
    scale = jnp.where(is_ctx, m_ref[0, 0, 4:5, :], m_ref[0, 1, 4:5, :])
    h = _modulate(x_ref[0], g_ref[...], shift, scale)
    hx_ref[0, :, :dm] = h
    logits = lax.dot_general(rwt_ref[...], h, (((1,), (1,)), ((), ())), precision=HIGHEST,
                             preferred_element_type=F32)
    scores = _sigmoid(logits)
    sel = scores + rb_ref[...]
    sel_r = [sel[e:e + 1, :] for e in range(N_EXPERTS)]
    sc_r = [scores[e:e + 1, :] for e in range(N_EXPERTS)]

    grp = []
    for g in range(N_GROUPS):
        a, b, c, d = sel_r[GROUP_SIZE * g:GROUP_SIZE * (g + 1)]
        hi1, lo1, hi2, lo2 = jnp.maximum(a, b), jnp.minimum(a, b), jnp.maximum(c, d), jnp.minimum(c, d)
        grp.append(jnp.maximum(hi1, hi2) + jnp.maximum(jnp.minimum(hi1, hi2), jnp.maximum(lo1, lo2)))
    best = jnp.zeros((1, tm), jnp.int32)
    cur = grp[0]
    for g in range(1, N_GROUPS):
        upd = grp[g] > cur
        best = jnp.where(upd, g, best)
        cur = jnp.where(upd, grp[g], cur)

    def in_group(rows, j):
        out = rows[j]
        for g in range(1, N_GROUPS):
            out = jnp.where(best == g, rows[GROUP_SIZE * g + j], out)
        return out

    v = [in_group(sel_r, j) for j in range(GROUP_SIZE)]
    sc = [in_group(sc_r, j) for j in range(GROUP_SIZE)]

    def first_argmax(vals):
        idx = jnp.zeros((1, tm), jnp.int32)
        top = vals[0]
        for j in range(1, GROUP_SIZE):
            upd = vals[j] > top
            idx = jnp.where(upd, j, idx)
            top = jnp.where(upd, vals[j], top)
        return idx

    i1 = first_argmax(v)
    i2 = first_argmax([jnp.where(i1 == j, -jnp.inf, v[j]) for j in range(GROUP_SIZE)])
    w1 = sum(jnp.where(i1 == j, sc[j], 0.0) for j in range(GROUP_SIZE))
    w2 = sum(jnp.where(i2 == j, sc[j], 0.0) for j in range(GROUP_SIZE))
    den = w1 + w2
    first_lo = i1 < i2
    lo = jnp.minimum(i1, i2)
    hi = jnp.maximum(i1, i2)
    w_lo = jnp.where(first_lo, w1, w2) / den
    w_hi = jnp.where(first_lo, w2, w1) / den
    pair = jnp.where(lo == 0, hi - 1, jnp.where(lo == 1, hi + 1, 5))
    cls = best * 6 + pair

    crow = lax.broadcasted_iota(jnp.int32, (CLASS_ROWS, tm), 0)
    onehot = crow == cls
    tri = (lax.broadcasted_iota(jnp.int32, (tm, tm), 0) <= lax.broadcasted_iota(jnp.int32, (tm, tm), 1))
    prefix = _bdot(jnp.where(onehot, 1.0, 0.0).astype(BF16), jnp.where(tri, 1.0, 0.0).astype(BF16))
    run = run_ref[...]
    rank = jnp.sum(jnp.where(onehot, prefix - 1.0 + run[:, 0:1], 0.0), axis=0, keepdims=True)
    run = run + prefix[:, tm - 1:tm]
    run_ref[...] = run
    counts_ref[...] = run

    r8 = lax.broadcasted_iota(jnp.int32, (8, tm), 0)
    meta_ref[...] = jnp.where(r8 == 0, cls.astype(F32), jnp.where(r8 == 1, rank, 0.0))
    rl = lax.broadcasted_iota(jnp.int32, (LANES, tm), 0)
    wt = jnp.where(rl == 0, w_lo, jnp.where(rl == 1, w_hi, 0.0))
    hx_ref[0, :, dm:] = wt.T


def _router(x, modsel, g, router_w, router_bias, lc):
    b, s, d = x.shape
    tm = ROUTER_TILE if s % ROUTER_TILE == 0 else ROW_TILE
    nt = s // tm
    return pl.pallas_call(
        functools.partial(_router_kernel, lc=lc),
        grid=(b, nt),
        in_specs=[_row_spec(tm, d), _mod_spec(d), _full_spec((1, d)), _full_spec((N_EXPERTS, d)),
                  _full_spec((N_EXPERTS, 1))],
        out_specs=[_row_spec(tm, d + LANES), pl.BlockSpec((8, tm), lambda bb, i: (0, bb * nt + i)),
                   _full_spec((CLASS_ROWS, LANES))],
        out_shape=[jax.ShapeDtypeStruct((b, s, d + LANES), F32), jax.ShapeDtypeStruct((8, b * s), F32),
                   jax.ShapeDtypeStruct((CLASS_ROWS, LANES), F32)],
        scratch_shapes=[pltpu.VMEM((CLASS_ROWS, LANES), F32)],
        compiler_params=_cparams("arbitrary", "arbitrary"),
        name="moe_router",
    )(x, modsel, g.reshape(1, d), router_w.T, router_bias.reshape(N_EXPERTS, 1))


def _row_copy(src, dst, sem):
    return pltpu.make_async_copy(src, dst, sem)


def _zero_fill(zpos_ref, zlen_ref, zbuf, xs_ref, sem, wait):
    for c in range(zpos_ref.shape[0]):
        units = zlen_ref[c]
        done = 0
        bit = zbuf.shape[0] // SUBLANES
        while bit >= 1:
            take = units & bit

            @pl.when(take != 0)
            def _(rows=bit * SUBLANES, done=done):
                start = zpos_ref[c] + done * SUBLANES
                cp = _row_copy(zbuf.at[pl.ds(0, rows)], xs_ref.at[pl.ds(start, rows)], sem)
                cp.wait() if wait else cp.start()

            done = done + take
            bit //= 2


def _dispatch_kernel(zpos_ref, zlen_ref, pos_ref, hx_ref, xs_ref, stage, zbuf, sem):
    tm = hx_ref.shape[0]

    @pl.when(pl.program_id(0) == 0)
    def _():
        zbuf[...] = jnp.zeros_like(zbuf)
        _zero_fill(zpos_ref, zlen_ref, zbuf, xs_ref, sem, False)
        _zero_fill(zpos_ref, zlen_ref, zbuf, xs_ref, sem, True)

    stage[...] = hx_ref[...].reshape(stage.shape)

    def start(i, carry):
        for k in range(SUBLANES):
            r = i * SUBLANES + k
            _row_copy(stage.at[r], xs_ref.at[pos_ref[0, 0, r]], sem).start(priority=k % 2)
        return carry

    lax.fori_loop(0, tm // SUBLANES, start, 0)
    _row_copy(stage, xs_ref.at[pl.ds(0, tm)], sem).wait()


def _dispatch(hx, pos, zpos, zlen, n_rows):
    n, dx = hx.shape
    tm = DISPATCH_TILE
    slab = (dx // LANES, LANES)
    return pl.pallas_call(
        _dispatch_kernel,
        grid_spec=pltpu.PrefetchScalarGridSpec(
            num_scalar_prefetch=2,
            grid=(n // tm,),
            in_specs=[pl.BlockSpec((1, 1, tm), lambda i, zp, zl: (i, 0, 0), memory_space=pltpu.SMEM),
                      pl.BlockSpec((tm, dx), lambda i, zp, zl: (i, 0))],
            out_specs=pl.BlockSpec(memory_space=pl.ANY),
            scratch_shapes=[pltpu.VMEM((tm,) + slab, F32), pltpu.VMEM((EXP_TILE,) + slab, F32),
                            pltpu.SemaphoreType.DMA(())],
        ),
        out_shape=jax.ShapeDtypeStruct((n_rows,) + slab, F32),
        compiler_params=_cparams("arbitrary"),
        name="moe_dispatch",
    )(zpos, zlen, pos.reshape(n // tm, 1, tm), hx)


def _experts_kernel(tblk_ref, elo_ref, ehi_ref, valid_ref, xs_ref, w1l, w3l, w2l, w1h, w3h, w2h, ys_ref):
    j = pl.program_id(0)
    tm = xs_ref.shape[0]
    d = ys_ref.shape[1] * ys_ref.shape[2]

    @pl.when(valid_ref[j] == 1)
    def _():
        xw = xs_ref[...].reshape(tm, xs_ref.shape[1] * xs_ref.shape[2])
        x = xw[:, :d].astype(BF16)

        def hidden(w1, w3, w):
            a = _bdot(x, w1[0])
            return ((a * _sigmoid(a)) * _bdot(x, w3[0]) * w).astype(BF16)

        y = (_bdot(hidden(w1l, w3l, xw[:, d:d + 1]), w2l[0]) + _bdot(hidden(w1h, w3h, xw[:, d + 1:d + 2]), w2h[0]))
        ys_ref[...] = y.reshape(ys_ref.shape)

    @pl.when(valid_ref[j] == 0)
    def _():
        ys_ref[...] = jnp.zeros_like(ys_ref)


def _experts(xs, tblk, elo, ehi, valid, w1, w3, w2, n_tiles):
    ne, d, f = w1.shape
    tm = EXP_TILE
    wlo = lambda shape: pl.BlockSpec(shape, lambda j, tb, lo, hi, va: (lo[j], 0, 0))
    whi = lambda shape: pl.BlockSpec(shape, lambda j, tb, lo, hi, va: (hi[j], 0, 0))
    return pl.pallas_call(
        _experts_kernel,
        grid_spec=pltpu.PrefetchScalarGridSpec(
            num_scalar_prefetch=4,
            grid=(n_tiles,),
            in_specs=[pl.BlockSpec((tm,) + xs.shape[1:], lambda j, tb, lo, hi, va: (tb[j], 0, 0)),
                      wlo((1, d, f)), wlo((1, d, f)), wlo((1, f, d)), whi((1, d, f)), whi((1, d, f)), whi((1, f, d))],
            out_specs=pl.BlockSpec((tm, d // LANES, LANES), lambda j, tb, lo, hi, va: (j, 0, 0)),
        ),
        out_shape=jax.ShapeDtypeStruct((n_tiles * tm, d // LANES, LANES), F32),
        compiler_params=_cparams("arbitrary"),
        name="moe_experts",
    )(tblk, elo, ehi, valid, xs, w1, w3, w2, w1, w3, w2)


def _combine_kernel(pos_ref, posn_ref, x_ref, m_ref, ys_ref, o_ref, buf, sem, *, nt, nctx, skip):
    t = pl.program_id(0)
    n = pl.num_programs(0)
    tm = x_ref.shape[1]
    slot = t % 2

    def gather(p_ref, sl):
        def body(i, carry):
            for k in range(SUBLANES):
                r = i * SUBLANES + k
                _row_copy(ys_ref.at[p_ref[0, 0, r]], buf.at[sl, r], sem.at[sl]).start(priority=k % 2)
            return carry
        lax.fori_loop(0, tm // SUBLANES, body, 0)

    @pl.when(t == 0)
    def _():
        gather(pos_ref, 0)

    @pl.when(t + 1 < n)
    def _():
        gather(posn_ref, 1 - slot)

    _row_copy(ys_ref.at[pl.ds(0, tm)], buf.at[slot], sem.at[slot]).wait()
    seg = ((t % nt) + skip >= nctx).astype(jnp.int32)
    o_ref[0] = x_ref[0] + m_ref[0, seg, 5:6, :] * buf[slot].reshape(tm, x_ref.shape[2])


def _combine(x, modsel, ys, pos, lc, latent_only):
    b, s, d = x.shape
    tm = ROW_TILE
    nctx = lc // tm
    skip = nctx if latent_only else 0
    nt = s // tm - skip
    n = b * nt
    pos3 = pos.reshape(b, s // tm, tm)[:, skip:].reshape(n, 1, tm)
    return pl.pallas_call(
        functools.partial(_combine_kernel, nt=nt, nctx=nctx, skip=skip),
        grid=(n,),
        in_specs=[pl.BlockSpec((1, 1, tm), lambda t: (t, 0, 0), memory_space=pltpu.SMEM),
                  pl.BlockSpec((1, 1, tm), lambda t: (jnp.minimum(t + 1, n - 1), 0, 0), memory_space=pltpu.SMEM),
                  pl.BlockSpec((1, tm, d), lambda t: (t // nt, t % nt + skip, 0)),
                  pl.BlockSpec((1, 2, 6, d), lambda t: (t // nt, 0, 0, 0)),
                  pl.BlockSpec(memory_space=pl.ANY)],
        out_specs=pl.BlockSpec((1, tm, d), lambda t: (t // nt, t % nt, 0)),
        out_shape=jax.ShapeDtypeStruct((b, nt * tm, d), F32),
        scratch_shapes=[pltpu.VMEM((2, tm) + ys.shape[1:], F32), pltpu.SemaphoreType.DMA((2,))],
        compiler_params=_cparams("arbitrary"),
        name="moe_combine",
    )(pos3, pos3, x, modsel, ys)


def _class_experts():
    lo, hi = [], []
    for g in range(N_GROUPS):
        for a in range(GROUP_SIZE):
            for c in range(a + 1, GROUP_SIZE):
                lo.append(GROUP_SIZE * g + a)
                hi.append(GROUP_SIZE * g + c)
    return jnp.array(lo, jnp.int32), jnp.array(hi, jnp.int32)


def _moe(x, modsel, g, router_w, router_bias, w1, w3, w2, lc, latent_only):
    b, s, d = x.shape
    n = b * s
    tm = EXP_TILE
    hx, meta, counts = _router(x, modsel, g, router_w, router_bias, lc)

    cnt = counts[:N_CLASSES, 0].astype(jnp.int32)
    padded = (cnt + tm - 1) // tm * tm
    ends = jnp.cumsum(padded)
    off = ends - padded
    pos = off[meta[0].astype(jnp.int32)] + meta[1].astype(jnp.int32)
    n_tiles = n // tm + N_CLASSES
    start = jnp.arange(n_tiles, dtype=jnp.int32) * tm
    valid = start < ends[-1]
    last = ends[-1] // tm - 1
    tblk = jnp.minimum(jnp.arange(n_tiles, dtype=jnp.int32), last)
    tcls = jnp.minimum(jnp.sum(ends[None, :] <= (tblk * tm)[:, None], axis=1), N_CLASSES - 1).astype(jnp.int32)
    lo_tab, hi_tab = _class_experts()

    n_rows = n_tiles * tm
    pad0 = (off + cnt) // SUBLANES * SUBLANES
    tail = ends[-1] + jnp.arange(N_CLASSES, dtype=jnp.int32) * tm
    zpos = jnp.concatenate([pad0, jnp.minimum(tail, n_rows - tm)]).astype(jnp.int32)
    zlen = jnp.concatenate([(ends - pad0) // SUBLANES, jnp.where(tail < n_rows, tm // SUBLANES, 0)]).astype(jnp.int32)
    xs = _dispatch(hx.reshape(n, d + LANES), pos, zpos, zlen, n_rows)
    ys = _experts(xs, tblk, lo_tab[tcls], hi_tab[tcls], valid.astype(jnp.int32), w1, w3, w2, n_tiles)
    return _combine(x, modsel, ys, pos, lc, latent_only)


def _rope_tables(lc, l):
    half = MLA_ROPE // 4
    freqs = ROPE_BASE ** (-jnp.arange(half, dtype=F32) / half)
    pos = jnp.arange(l)
    pos_row = (pos // GRID_W).astype(F32)
    pos_col = (pos % GRID_W).astype(F32)
    ang = jnp.concatenate([pos_row[:, None] * freqs, pos_row[:, None] * freqs,
                           pos_col[:, None] * freqs, pos_col[:, None] * freqs], axis=1)
    pad = jnp.zeros((l, 128 - MLA_ROPE), F32)
    cos = jnp.concatenate([jnp.concatenate([jnp.ones((lc, MLA_ROPE), F32), jnp.zeros((lc, 128 - MLA_ROPE), F32)], axis=1),
                           jnp.concatenate([jnp.cos(ang), pad], axis=1)], axis=0)
    sin = jnp.concatenate([jnp.zeros((lc, 128), F32), jnp.concatenate([jnp.sin(ang), pad], axis=1)], axis=0)
    r = jnp.zeros((128, 128), F32)
    idx = jnp.arange(half)
    for base in (0, MLA_ROPE // 2):
        r = r.at[base + half + idx, base + idx].set(-1.0)
        r = r.at[base + idx, base + half + idx].set(1.0)
    return cos, sin, r.astype(BF16)


def _pad_heads(w, heads):
    k = w.shape[0]
    w = w.reshape(k, heads, MLA_NOPE + MLA_ROPE)
    w = jnp.concatenate([w, jnp.zeros((k, heads, MLA_QK - MLA_NOPE - MLA_ROPE), w.dtype)], axis=2)
    return w.reshape(k, heads * MLA_QK)


def kernel(x, c, ctx, c_ctx, mod_w, mod_b, norm1_g, norm2_g, hyb_in_w, hyb_out_w, s5_a_re, s5_a_im, s5_log_dt, s5_b_re, s5_b_im, s5_c_re, s5_c_im, s5_d, s5_glu_w, s5_glu_b, gm_ln_g, gm_ln_b, gm_ws, gm_bs, mla_in_w, mla_q_norm_g, mla_kv_norm_g, mla_wq_b, mla_wkv_b, mla_qn_g, mla_qr_g, mla_kn_g, mla_kr_g, mla_out_w, router_w, router_bias, moe_w1, moe_w3, moe_w2):
    b, l, d = x.shape
    lc = ctx.shape[1]
    depth = mod_w.shape[0]
    heads = mla_out_w.shape[1] // MLA_V
    assert lc % ROW_TILE == 0 and l % ROW_TILE == 0 and b % S5_BATCH_TILE == 0

    xs = jnp.concatenate([ctx, x], axis=1)

    bp = -(-(b + 1) // 8) * 8
    cond = jnp.concatenate([c, c_ctx[None], jnp.zeros((bp - b - 1, d), F32)], axis=0)
    mods = _adaln(cond, mod_w, mod_b)

    cos, sin, rot = _rope_tables(lc, l)

    for layer in range(depth):
        i = layer // 2
        m = mods[layer].reshape(bp, 6, d)
        modsel = jnp.stack([jnp.broadcast_to(m[b][None], (b, 6, d)), m[:b]], axis=1)

        if layer % 2 == 0:
            u, zg = _hyb_in(xs, modsel, norm1_g[layer], hyb_in_w[i].astype(BF16), lc)
            ab_re, ab_im, bb_re, bb_im = _s5_discretize(s5_a_re[i], s5_a_im[i], s5_log_dt[i], s5_b_re[i], s5_b_im[i])
            ys = []
            for dr in range(2):
                mats = _s5_matrices(ab_re[dr], ab_im[dr], bb_re[dr], bb_im[dr], s5_c_re[i, dr], s5_c_im[i, dr])
                ys.append(_s5_scan(u, mats, lc, reverse=bool(dr)))
            hd = (d // 2) // GM_HEADS
            bsx = jnp.repeat(gm_bs[i].T, hd, axis=1)
            xs = _hyb_mid(xs, modsel, u, ys[0], ys[1], zg, s5_d[i], s5_glu_w[i].astype(BF16), s5_glu_b[i],
                          gm_ln_g[i], gm_ln_b[i], gm_ws[i].astype(BF16), bsx, hyb_out_w[i].astype(BF16), lc)
        else:
            q_lora = mla_q_norm_g.shape[1]
            kv_lora = mla_kv_norm_g.shape[1]
            win = jnp.concatenate([mla_in_w[i], jnp.zeros((d, 128 - MLA_ROPE), F32)], axis=1).astype(BF16)
            wq = _pad_heads(mla_wq_b[i], heads).astype(BF16)
            q, k, v = _mla_in(xs, modsel, norm1_g[layer], win, mla_q_norm_g[i], mla_kv_norm_g[i], wq,
                              mla_wkv_b[i].astype(BF16), mla_qn_g[i], mla_qr_g[i], mla_kn_g[i], mla_kr_g[i],
                              rot, cos, sin, lc, heads)
            o = _attention(q, k, v, lc)
            xs = _mla_out(xs, modsel, o, mla_out_w[i].astype(BF16), lc)

        xs = _moe(xs, modsel, norm2_g[layer], router_w, router_bias, moe_w1[layer].astype(BF16),
                  moe_w3[layer].astype(BF16), moe_w2[layer].astype(BF16), lc, latent_only=layer == depth - 1)

    return xs
```
